```python
import jax, jax.numpy as jnp
from jax import lax
import numpy as np

D_MODEL = 1024
BATCH = 4
SEQ = 4096
DEPTH = 2
DEC_BATCH = 128
DEC_SEQ = 8
PAST_LEN = 2048
PAGE_SIZE = 128

HEAD_DIM = 64
POOL_WINDOWS = (2, 4, 8, 16)
POOL_WIDTH = D_MODEL // 4
POOL_GROUP = POOL_WIDTH // len(POOL_WINDOWS)
POOL_KEEP = max(POOL_WINDOWS) - 1
SSD_HEADS = (3 * D_MODEL // 8) // HEAD_DIM
SSD_WIDTH = SSD_HEADS * HEAD_DIM
SSD_GROUPS = 2
D_STATE = 64
CONV_W = 4
CONV_DIM = SSD_WIDTH + 2 * SSD_GROUPS * D_STATE
SSD_CHUNK = 128
ATT_HEADS = (D_MODEL - POOL_WIDTH - SSD_WIDTH) // HEAD_DIM
ATT_WIDTH = ATT_HEADS * HEAD_DIM
MIX_WIDTH = POOL_WIDTH + SSD_WIDTH + ATT_WIDTH
DILATED = ((128, 1), (512, 4), (2048, 16))
MAX_WINDOW = 2048
ATT_BLOCK = 128
IN_WIDTH = POOL_WIDTH + SSD_WIDTH + CONV_DIM + SSD_HEADS + 3 * ATT_WIDTH
N_EXPERTS = 32
TOP_K = 4
D_FF = D_MODEL
SWIGLU_LIMIT = 7.0
SWIGLU_ALPHA = 1.702
MOE_BLOCK = 256
EPS = 1e-6

kernel_name = 'hymba_pool_ssd_dilated_moe_step'


def _in_splits():
    widths = (POOL_WIDTH, SSD_WIDTH, CONV_DIM, SSD_HEADS, ATT_WIDTH, ATT_WIDTH)
    return [int(v) for v in np.cumsum(widths)]


def rms_norm(x, g):
    x32 = x.astype(jnp.float32)
    y = x32 * lax.rsqrt(jnp.mean(x32 * x32, axis=-1, keepdims=True) + EPS)
    return (y * g.astype(jnp.float32)).astype(x.dtype)


def pool_mixer(u, u_past, start_pos, pool_w, pool_scale):
    b, L, _ = u.shape
    n_past = u_past.shape[1]
    full = jnp.concatenate([u_past, u], axis=1).astype(jnp.float32)
    cs = jnp.concatenate([jnp.zeros_like(full[:, :1]), jnp.cumsum(full, axis=1)], axis=1)
    idx = n_past + jnp.arange(L)
    pos = start_pos + jnp.arange(L)
    pooled = []
    for g, w in enumerate(POOL_WINDOWS):
        sl = slice(g * POOL_GROUP, (g + 1) * POOL_GROUP)
        lo = jnp.maximum(idx + 1 - w, 0)
        cnt = jnp.minimum(pos + 1, w).astype(jnp.float32)
        pooled.append((cs[:, idx + 1, sl] - cs[:, lo, sl]) / cnt[None, :, None])
    diff = (jnp.concatenate(pooled, axis=-1) - full[:, n_past:]).reshape(b, L, len(POOL_WINDOWS), POOL_GROUP)
    out = jnp.einsum('blgc,gcd->blgd', diff, pool_w.astype(jnp.float32)).reshape(b, L, POOL_WIDTH)
    out = out * pool_scale.astype(jnp.float32)
    return out.astype(u.dtype), full[:, -POOL_KEEP:].astype(u.dtype)


def ssd_scan(x, dt, a, bm, cm, h0, chunk):
    b, L, H, P = x.shape
    G, N = bm.shape[2], bm.shape[3]
    R = H // G
    nc = L // chunk
    xc = x.reshape(b, nc, chunk, G, R, P)
    dtc = dt.reshape(b, nc, chunk, G, R)
    bc = bm.reshape(b, nc, chunk, G, N).astype(jnp.float32)
    cc = cm.reshape(b, nc, chunk, G, N).astype(jnp.float32)
    acum = jnp.cumsum(dtc * a.reshape(G, R), axis=2)
    causal = jnp.tril(jnp.ones((chunk, chunk), bool))[:, :, None, None]
    seg = acum[:, :, :, None] - acum[:, :, None, :]
    decay = jnp.exp(jnp.where(causal, seg, -jnp.inf))
    cb = jnp.einsum('bctgn,bcsgn->bctsg', cc, bc)
    mat = cb[..., None] * decay * dtc[:, :, None]
    y_diag = jnp.einsum('bctsgr,bcsgrp->bctgrp', mat, xc)
    decay_end = jnp.exp(acum[:, :, -1:] - acum)
    states = jnp.einsum('bcsgn,bcsgr,bcsgrp->bcgrpn', bc, decay_end * dtc, xc)
    chunk_decay = jnp.exp(acum[:, :, -1])

    def step(h, inp):
        dec, st = inp
        return dec[..., None, None] * h + st, h

    h_final, h_starts = lax.scan(step, h0.reshape(b, G, R, P, N),
                                 (jnp.moveaxis(chunk_decay, 1, 0), jnp.moveaxis(states, 1, 0)))
    h_starts = jnp.moveaxis(h_starts, 0, 1)
    y_off = jnp.einsum('bctgn,bctgr,bcgrpn->bctgrp', cc, jnp.exp(acum), h_starts)
    return (y_diag + y_off).reshape(b, L, H, P), h_final.reshape(b, H, P, N)


def ssd_mixer(z, xbc, dt_raw, conv_past, h0, chunk, conv_w, conv_b, dt_bias, a_log, d_skip, norm_g):
    b, L, _ = xbc.shape
    full = jnp.concatenate([conv_past, xbc], axis=1)
    acc = conv_b.astype(jnp.float32)
    for j in range(CONV_W):
        acc = acc + full[:, j:j + L].astype(jnp.float32) * conv_w[j].astype(jnp.float32)
    xbc_c = jax.nn.silu(acc)
    nb = SSD_WIDTH + SSD_GROUPS * D_STATE
    xs = xbc_c[..., :SSD_WIDTH].reshape(b, L, SSD_HEADS, HEAD_DIM)
    bm = xbc_c[..., SSD_WIDTH:nb].reshape(b, L, SSD_GROUPS, D_STATE)
    cm = xbc_c[..., nb:].reshape(b, L, SSD_GROUPS, D_STATE)
    dt = jax.nn.softplus(dt_raw.astype(jnp.float32) + dt_bias.astype(jnp.float32))
    a = -jnp.exp(a_log.astype(jnp.float32))
    y, h = ssd_scan(xs, dt, a, bm, cm, h0.astype(jnp.float32), chunk)
    y = (y + d_skip.astype(jnp.float32)[:, None] * xs).reshape(b, L, SSD_WIDTH)
    y = rms_norm(y * jax.nn.silu(z.astype(jnp.float32)), norm_g)
    return y.astype(z.dtype), full[:, -(CONV_W - 1):], h.astype(z.dtype)


def band_attention(q, k, v, steps):
    n, L, H, hd = q.shape
    nb = -(-L // ATT_BLOCK)
    pad = nb * ATT_BLOCK - L

    def blocks(t):
        return jnp.pad(t, ((0, 0), (0, pad), (0, 0), (0, 0))).reshape(n, nb, ATT_BLOCK, H, hd)

    def with_prev(t):
        return jnp.concatenate([jnp.concatenate([jnp.zeros_like(t[:, :1]), t[:, :-1]], axis=1), t], axis=2)

    qb = blocks(q)
    kk = with_prev(blocks(k))
    vv = with_prev(blocks(v))
    s = jnp.einsum('nbqhd,nbkhd->nbhqk', qb, kk, preferred_element_type=jnp.float32) * (HEAD_DIM ** -0.5)
    qi = jnp.arange(ATT_BLOCK)[:, None] + ATT_BLOCK
    kj = jnp.arange(2 * ATT_BLOCK)[None, :]
    dist = qi - kj
    blk = jnp.arange(nb)[:, None, None]
    valid = (dist >= 0) & (dist <= steps) & (blk * ATT_BLOCK - ATT_BLOCK + kj >= 0)
    s = jnp.where(valid[None, :, None], s, -jnp.inf)
    m = jnp.max(s, axis=-1, keepdims=True)
    p = jnp.exp(s - m)
    den = jnp.sum(p, axis=-1)
    o = jnp.einsum('nbhqk,nbkhd->nbqhd', p, vv.astype(jnp.float32))
    den_t = jnp.swapaxes(den, 2, 3)
    o = o / den_t[..., None]
    lse = jnp.swapaxes(m[..., 0], 2, 3) + jnp.log(den_t)
    return o.reshape(n, nb * ATT_BLOCK, H, hd)[:, :L], lse.reshape(n, nb * ATT_BLOCK, H)[:, :L]


def combine_dilations(outs, lses):
    wts = jax.nn.softmax(jnp.stack(lses, axis=0), axis=0)
    return jnp.einsum('kbsh,kbshd->bshd', wts, jnp.stack(outs, axis=0))


def dilated_attention_prompt(q, k, v):
    b, S, H, hd = q.shape
    outs, lses = [], []
    for w, d in DILATED:
        Qd = S // d

        def fold(t):
            return t.reshape(b, Qd, d, H, hd).transpose(0, 2, 1, 3, 4).reshape(b * d, Qd, H, hd)

        o, lse = band_attention(fold(q), fold(k), fold(v), w // d)
        outs.append(o.reshape(b, d, Qd, H, hd).transpose(0, 2, 1, 3, 4).reshape(b, S, H, hd))
        lses.append(lse.reshape(b, d, Qd, H).transpose(0, 2, 1, 3).reshape(b, S, H))
    return combine_dilations(outs, lses)


def dilated_attention_sample(q, k_all, v_all):
    b, T = q.shape[0], q.shape[1]
    n_past = k_all.shape[1] - T
    qpos = n_past + jnp.arange(T)
    outs, lses = [], []
    for w, d in DILATED:
        kidx = qpos[:, None] - d * jnp.arange(w // d + 1)[None, :]
        valid = kidx >= 0
        kidx = jnp.maximum(kidx, 0)
        kg = k_all[:, kidx]
        vg = v_all[:, kidx]
        s = jnp.einsum('bthd,btjhd->bthj', q, kg, preferred_element_type=jnp.float32) * (HEAD_DIM ** -0.5)
        s = jnp.where(valid[None, :, None, :], s, -jnp.inf)
        m = jnp.max(s, axis=-1, keepdims=True)
        p = jnp.exp(s - m)
        den = jnp.sum(p, axis=-1)
        o = jnp.einsum('bthj,btjhd->bthd', p, vg.astype(jnp.float32)) / den[..., None]
        outs.append(o)
        lses.append(m[..., 0] + jnp.log(den))
    return combine_dilations(outs, lses)


def moe_ffn(h, router_w, router_b, w_gate_up, b_gate_up, w_down, b_down):
    T, D = h.shape
    logits = jnp.matmul(h, router_w, preferred_element_type=jnp.float32) + router_b.astype(jnp.float32)
    top_v, top_i = lax.top_k(logits, TOP_K)
    gates = jax.nn.softmax(top_v, axis=-1)
    n_assign = T * TOP_K
    flat_e = top_i.reshape(-1)
    flat_tok = jnp.arange(n_assign) // TOP_K
    order = jnp.argsort(flat_e)
    se, stok, sg = flat_e[order], flat_tok[order], gates.reshape(-1)[order]
    counts = jnp.bincount(flat_e, length=N_EXPERTS)
    starts = jnp.cumsum(counts) - counts
    pcounts = (counts + MOE_BLOCK - 1) // MOE_BLOCK * MOE_BLOCK
    pends = jnp.cumsum(pcounts)
    pstarts = pends - pcounts
    dest = pstarts[se] + (jnp.arange(n_assign) - starts[se])
    n_blocks = -(-n_assign // MOE_BLOCK) + N_EXPERTS
    n_rows = n_blocks * MOE_BLOCK
    rows_tok = jnp.full((n_rows,), T, jnp.int32).at[dest].set(stok.astype(jnp.int32))
    rows_g = jnp.zeros((n_rows,), jnp.float32).at[dest].set(sg)
    block_e = jnp.clip(jnp.searchsorted(pends, jnp.arange(n_blocks) * MOE_BLOCK, side='right'), 0, N_EXPERTS - 1)
    hp = jnp.concatenate([h, jnp.zeros((1, D), h.dtype)], axis=0)
    xb = hp[rows_tok].reshape(n_blocks, MOE_BLOCK, D)

    def expert_block(args):
        xe, e = args
        gu = xe @ w_gate_up[e] + b_gate_up[e]
        gate = jnp.minimum(gu[:, :D_FF], SWIGLU_LIMIT)
        up = jnp.clip(gu[:, D_FF:], -SWIGLU_LIMIT, SWIGLU_LIMIT)
        act = (up + 1.0) * gate * jax.nn.sigmoid(SWIGLU_ALPHA * gate)
        return act @ w_down[e] + b_down[e]

    yb = lax.map(expert_block, (xb, block_e)).reshape(n_rows, D)
    out = jnp.zeros((T + 1, D), jnp.float32).at[rows_tok].add(yb.astype(jnp.float32) * rows_g[:, None])
    return out[:T].astype(h.dtype)


def layer(x, c, l, p, past, prompt):
    b, L, _ = x.shape
    mod = jax.nn.silu(c) @ p['w_ada'][l] + p['b_ada'][l]
    sh1, sc1, gt1, sh2, sc2, gt2 = jnp.split(mod[:, None, :], 6, axis=-1)
    h = rms_norm(x, p['g_mix'][l]) * (1 + sc1) + sh1
    u, z, xbc, dt_raw, q, k, v = jnp.split(h @ p['w_in'][l], _in_splits(), axis=-1)
    if prompt:
        pool_past = jnp.zeros((b, 0, POOL_WIDTH), x.dtype)
        conv_past = jnp.zeros((b, CONV_W - 1, CONV_DIM), x.dtype)
        h0 = jnp.zeros((b, SSD_HEADS, HEAD_DIM, D_STATE), jnp.float32)
        start, chunk = 0, SSD_CHUNK
    else:
        pool_past, conv_past, h0, k_past, v_past = past
        start, chunk = PAST_LEN, L
    pool_out, pool_new = pool_mixer(u, pool_past, start, p['pool_w'][l], p['pool_scale'][l])
    ssd_out, conv_new, ssm_new = ssd_mixer(z, xbc, dt_raw, conv_past, h0, chunk, p['conv_w'][l], p['conv_b'][l],
                                           p['dt_bias'][l], p['a_log'][l], p['d_skip'][l], p['ssd_norm_g'][l])
    q = q.reshape(b, L, ATT_HEADS, HEAD_DIM)
    k = k.reshape(b, L, ATT_HEADS, HEAD_DIM)
    v = v.reshape(b, L, ATT_HEADS, HEAD_DIM)
    if prompt:
        att = dilated_attention_prompt(q, k, v)
        keep = min(MAX_WINDOW, L)
        k_new, v_new = k[:, -keep:], v[:, -keep:]
    else:
        k_all = jnp.concatenate([k_past, k], axis=1)
        v_all = jnp.concatenate([v_past, v], axis=1)
        att = dilated_attention_sample(q, k_all, v_all)
        keep = k_past.shape[1]
        k_new, v_new = k_all[:, -keep:], v_all[:, -keep:]
    mix = jnp.concatenate([pool_out, ssd_out, att.reshape(b, L, ATT_WIDTH).astype(x.dtype)], axis=-1) @ p['w_out'][l]
    x = x + gt1 * mix
    h2 = rms_norm(x, p['g_ffn'][l]) * (1 + sc2) + sh2
    ff = moe_ffn(h2.reshape(b * L, D_MODEL), p['router_w'][l], p['router_b'][l], p['w_gate_up'][l],
                 p['b_gate_up'][l], p['w_down'][l], p['b_down'][l]).reshape(b, L, D_MODEL)
    x = x + gt2 * ff
    return x, (pool_new, conv_new, ssm_new, k_new, v_new)


def setup_inputs(seed: int = 0) -> dict:
    key = jax.random.key(seed)
    ks = jax.random.split(key, 32)
    f32 = jnp.float32

    def nrm(k, shape, scale):
        return jax.random.normal(k, shape, f32) * scale

    wbuf = min(MAX_WINDOW, PAST_LEN)
    dt0 = jnp.exp(jax.random.uniform(ks[20], (DEPTH, SSD_HEADS), f32, float(np.log(1e-3)), float(np.log(1e-1))))
    return {
        'x_prompt': nrm(ks[0], (BATCH, SEQ, D_MODEL), 1.0),
        'x_sample': nrm(ks[1], (DEC_BATCH, DEC_SEQ, D_MODEL), 1.0),
        'state_pool': nrm(ks[2], (DEPTH, DEC_BATCH, POOL_KEEP, POOL_WIDTH), 1.0),
        'state_conv': nrm(ks[3], (DEPTH, DEC_BATCH, CONV_W - 1, CONV_DIM), 1.0),
        'state_ssm': nrm(ks[4], (DEPTH, DEC_BATCH, SSD_HEADS, HEAD_DIM, D_STATE), 0.1),
        'cache_k': nrm(ks[5], (DEPTH, DEC_BATCH, wbuf, ATT_HEADS, HEAD_DIM), 1.0),
        'cache_v': nrm(ks[6], (DEPTH, DEC_BATCH, wbuf, ATT_HEADS, HEAD_DIM), 1.0),
        'c_prompt': nrm(ks[7], (BATCH, D_MODEL), 1.0),
        'c_sample': nrm(ks[8], (DEC_BATCH, D_MODEL), 1.0),
        'g_mix': 1.0 + nrm(ks[9], (DEPTH, D_MODEL), 0.02),
        'w_ada': nrm(ks[10], (DEPTH, D_MODEL, 6 * D_MODEL), 0.5 * D_MODEL ** -0.5),
        'b_ada': nrm(ks[11], (DEPTH, 6 * D_MODEL), 0.02),
        'w_in': nrm(ks[12], (DEPTH, D_MODEL, IN_WIDTH), D_MODEL ** -0.5),
        'pool_w': nrm(ks[13], (DEPTH, len(POOL_WINDOWS), POOL_GROUP, POOL_GROUP), POOL_GROUP ** -0.5),
        'pool_scale': 1.0 + nrm(ks[14], (DEPTH, POOL_WIDTH), 0.02),
        'conv_w': nrm(ks[15], (DEPTH, CONV_W, CONV_DIM), CONV_W ** -0.5),
        'conv_b': nrm(ks[16], (DEPTH, CONV_DIM), 0.02),
        'dt_bias': dt0 + jnp.log(-jnp.expm1(-dt0)),
        'a_log': jnp.log(jax.random.uniform(ks[17], (DEPTH, SSD_HEADS), f32, 1.0, 16.0)),
        'd_skip': 1.0 + nrm(ks[18], (DEPTH, SSD_HEADS), 0.02),
        'ssd_norm_g': 1.0 + nrm(ks[19], (DEPTH, SSD_WIDTH), 0.02),
        'w_out': nrm(ks[21], (DEPTH, MIX_WIDTH, D_MODEL), MIX_WIDTH ** -0.5),
        'g_ffn': 1.0 + nrm(ks[22], (DEPTH, D_MODEL), 0.02),
        'router_w': nrm(ks[23], (DEPTH, D_MODEL, N_EXPERTS), D_MODEL ** -0.5),
        'router_b': nrm(ks[24], (DEPTH, N_EXPERTS), 0.01),
        'w_gate_up': nrm(ks[25], (DEPTH, N_EXPERTS, D_MODEL, 2 * D_FF), D_MODEL ** -0.5),
        'b_gate_up': nrm(ks[26], (DEPTH, N_EXPERTS, 2 * D_FF), 0.02),
        'w_down': nrm(ks[27], (DEPTH, N_EXPERTS, D_FF, D_MODEL), D_FF ** -0.5),
        'b_down': nrm(ks[28], (DEPTH, N_EXPERTS, D_MODEL), 0.02),
        'g_final': 1.0 + nrm(ks[29], (D_MODEL,), 0.02),
    }


def reference(x_prompt, x_sample, state_pool, state_conv, state_ssm, cache_k, cache_v, c_prompt, c_sample,
              g_mix, w_ada, b_ada, w_in, pool_w, pool_scale, conv_w, conv_b, dt_bias, a_log, d_skip,
              ssd_norm_g, w_out, g_ffn, router_w, router_b, w_gate_up, b_gate_up, w_down, b_down, g_final):
    p = {'g_mix': g_mix, 'w_ada': w_ada, 'b_ada': b_ada, 'w_in': w_in, 'pool_w': pool_w,
         'pool_scale': pool_scale, 'conv_w': conv_w, 'conv_b': conv_b, 'dt_bias': dt_bias, 'a_log': a_log,
         'd_skip': d_skip, 'ssd_norm_g': ssd_norm_g, 'w_out': w_out, 'g_ffn': g_ffn, 'router_w': router_w,
         'router_b': router_b, 'w_gate_up': w_gate_up, 'b_gate_up': b_gate_up, 'w_down': w_down,
         'b_down': b_down}
    xp, xs = x_prompt, x_sample
    new_p, new_s = [], []
    for l in range(DEPTH):
        xp, st_p = layer(xp, c_prompt, l, p, None, True)
        past = (state_pool[l], state_conv[l], state_ssm[l], cache_k[l], cache_v[l])
        xs, st_s = layer(xs, c_sample, l, p, past, False)
        new_p.append(st_p)
        new_s.append(st_s)

    def stacked(sts, i):
        return jnp.stack([s[i] for s in sts], axis=0)

    y_prompt = rms_norm(xp, g_final)
    y_sample = rms_norm(xs, g_final)
    return (y_prompt, y_sample,
            stacked(new_p, 0), stacked(new_s, 0),
            stacked(new_p, 1), stacked(new_s, 1),
            stacked(new_p, 2), stacked(new_s, 2),
            stacked(new_p, 3), stacked(new_s, 3),
            stacked(new_p, 4), stacked(new_s, 4))
```

```python
import functools

import numpy as np
import jax
import jax.numpy as jnp
from jax import lax
from jax.experimental import pallas as pl
from jax.experimental.pallas import tpu as pltpu

F32 = jnp.float32
BF16 = jnp.bfloat16

D_MODEL = 1024
HEAD_DIM = 64
POOL_WINDOWS = (2, 4, 8, 16)
POOL_WIDTH = 256
POOL_GROUP = 64
POOL_KEEP = 15
SSD_HEADS = 6
SSD_WIDTH = 384
SSD_GROUPS = 2
HEADS_PER_GROUP = SSD_HEADS // SSD_GROUPS
D_STATE = 64
CONV_W = 4
CONV_DIM = 640
SSD_CHUNK = 128
ATT_HEADS = 6
ATT_WIDTH = 384
DILATED = ((128, 1), (512, 4), (2048, 16))
MAX_WINDOW = 2048
ATT_BLOCK = 128
N_EXPERTS = 32
TOP_K = 4
D_FF = 1024
SWIGLU_LIMIT = 7.0
SWIGLU_ALPHA = 1.702
EPS = 1e-6

LANES = 128
SUBLANES = 8
TOKEN_TILE = 256
EXPERT_BLOCK = 256
SEQS_PER_SSD_BLOCK = 16
VMEM_LIMIT = 48 * 1024 * 1024
NEG = -1e30

_IN_COLS = (("u", POOL_WIDTH), ("z", SSD_WIDTH), ("xbc", CONV_DIM), ("dt", LANES),
            ("q", ATT_WIDTH), ("k", ATT_WIDTH), ("v", ATT_WIDTH))
IN_PACKED = sum(w for _, w in _IN_COLS)


def _params(sem, vmem=VMEM_LIMIT):
    return pltpu.CompilerParams(dimension_semantics=sem, vmem_limit_bytes=vmem)


def _sigmoid(x):
    return 1.0 / (1.0 + jnp.exp(-x))


def _silu(x):
    return x * _sigmoid(x)


def _dot(a, b):
    return jnp.dot(a, b, preferred_element_type=F32)


def _dot_nt(a, b):
    return lax.dot_general(a, b, (((1,), (1,)), ((), ())), preferred_element_type=F32)


def _dot_exact(a, b):
    return jnp.dot(a, b, preferred_element_type=F32, precision=lax.Precision.HIGHEST)


def _rms(x, g):
    ms = jnp.mean(x * x, axis=-1, keepdims=True)
    return x * lax.rsqrt(ms + EPS) * g


def _ada_kernel(c_ref, w_ref, b_ref, o_ref):
    sc = _silu(c_ref[...]).astype(BF16)
    o_ref[0] = _dot(sc, w_ref[0].astype(BF16)) + b_ref[0]


def _ada(c_all, w_ada, b_ada):
    depth, _, n = w_ada.shape
    rows = c_all.shape[0]
    tn = 512
    return pl.pallas_call(
        _ada_kernel,
        grid=(depth, n // tn),
        in_specs=[pl.BlockSpec((rows, D_MODEL), lambda l, j: (0, 0)),
                  pl.BlockSpec((1, D_MODEL, tn), lambda l, j: (l, 0, j)),
                  pl.BlockSpec((1, 1, tn), lambda l, j: (l, 0, j))],
        out_specs=pl.BlockSpec((1, rows, tn), lambda l, j: (l, 0, j)),
        out_shape=jax.ShapeDtypeStruct((depth, rows, n), F32),
        compiler_params=_params(("arbitrary", "arbitrary")),
    )(c_all, w_ada, b_ada.reshape(depth, 1, n))


def _inproj_kernel(x_ref, sc_ref, sh_ref, g_ref, w_ref, u_ref, z_ref, xbc_ref, dt_ref, q_ref, k_ref, v_ref):
    h = _rms(x_ref[...], g_ref[...]) * (1.0 + sc_ref[...]) + sh_ref[...]
    r = _dot(h.astype(BF16), w_ref[...])
    off = 0
    for ref, (_, w) in zip((u_ref, z_ref, xbc_ref, dt_ref, q_ref, k_ref, v_ref), _IN_COLS):
        ref[...] = r[:, off:off + w]
        off += w


def _inproj(x, scx, shx, g, w_packed, mod_map):
    t = x.shape[0]
    tm = TOKEN_TILE
    tok = lambda i: (i, 0)
    const = lambda i: (0, 0)
    return pl.pallas_call(
        _inproj_kernel,
        grid=(t // tm,),
        in_specs=[pl.BlockSpec((tm, D_MODEL), tok),
                  pl.BlockSpec((tm, D_MODEL), mod_map),
                  pl.BlockSpec((tm, D_MODEL), mod_map),
                  pl.BlockSpec((1, D_MODEL), const),
                  pl.BlockSpec((D_MODEL, IN_PACKED), const)],
        out_specs=[pl.BlockSpec((tm, w), tok) for _, w in _IN_COLS],
        out_shape=[jax.ShapeDtypeStruct((t, w), F32) for _, w in _IN_COLS],
        compiler_params=_params(("arbitrary",)),
    )(x, scx, shx, g, w_packed)


def _mixout_kernel(pool_ref, ssd_ref, att_ref, x_ref, gt_ref, sc_ref, sh_ref, g_ref, wo_ref, rw_ref, rb_ref,
                   x1_ref, h2_ref, lg_ref):
    mix = _dot(pool_ref[...].astype(BF16), wo_ref[0:POOL_WIDTH, :])
    mix = mix + _dot(ssd_ref[...].astype(BF16), wo_ref[POOL_WIDTH:POOL_WIDTH + SSD_WIDTH, :])
    mix = mix + _dot(att_ref[...].astype(BF16), wo_ref[POOL_WIDTH + SSD_WIDTH:, :])
    x1 = x_ref[...] + gt_ref[...] * mix
    x1_ref[...] = x1
    h2 = _rms(x1, g_ref[...]) * (1.0 + sc_ref[...]) + sh_ref[...]
    h2_ref[...] = h2
    lg_ref[...] = _dot_exact(h2, rw_ref[...]) + rb_ref[...]


def _mixout(pool, ssd, att, x, gtx, scx, shx, g, wo, rw, rb, mod_map):
    t = x.shape[0]
    tm = TOKEN_TILE
    tok = lambda i: (i, 0)
    const = lambda i: (0, 0)
    return pl.pallas_call(
        _mixout_kernel,
        grid=(t // tm,),
        in_specs=[pl.BlockSpec((tm, POOL_WIDTH), tok),
                  pl.BlockSpec((tm, SSD_WIDTH), tok),
                  pl.BlockSpec((tm, ATT_WIDTH), tok),
                  pl.BlockSpec((tm, D_MODEL), tok),
                  pl.BlockSpec((tm, D_MODEL), mod_map),
                  pl.BlockSpec((tm, D_MODEL), mod_map),
                  pl.BlockSpec((tm, D_MODEL), mod_map),
                  pl.BlockSpec((1, D_MODEL), const),
                  pl.BlockSpec((D_MODEL, D_MODEL), const),
                  pl.BlockSpec((D_MODEL, LANES), const),
                  pl.BlockSpec((1, LANES), const)],
        out_specs=[pl.BlockSpec((tm, D_MODEL), tok),
                   pl.BlockSpec((tm, D_MODEL), tok),
                   pl.BlockSpec((tm, LANES), tok)],
        out_shape=[jax.ShapeDtypeStruct((t, D_MODEL), F32),
                   jax.ShapeDtypeStruct((t, D_MODEL), F32),
                   jax.ShapeDtypeStruct((t, LANES), F32)],
        compiler_params=_params(("arbitrary",)),
    )(pool, ssd, att, x, gtx, scx, shx, g, wo, rw, rb)


def _final_norm_kernel(x_ref, g_ref, o_ref):
    o_ref[...] = _rms(x_ref[...], g_ref[...])


def _final_norm(x, g):
    t = x.shape[0]
    tm = TOKEN_TILE
    return pl.pallas_call(
        _final_norm_kernel,
        grid=(t // tm,),
        in_specs=[pl.BlockSpec((tm, D_MODEL), lambda i: (i, 0)),
                  pl.BlockSpec((1, D_MODEL), lambda i: (0, 0))],
        out_specs=pl.BlockSpec((tm, D_MODEL), lambda i: (i, 0)),
        out_shape=jax.ShapeDtypeStruct((t, D_MODEL), F32),
        compiler_params=_params(("arbitrary",)),
    )(x, g)


_POOL_PAD = 8
_POOL_CARRY = 16


def _pool_select(sums, cnts, u):
    lane = lax.broadcasted_iota(jnp.int32, u.shape, u.ndim - 1)
    pooled = sums[-1] / cnts[-1]
    for g in range(len(POOL_WINDOWS) - 2, -1, -1):
        pooled = jnp.where(lane < (g + 1) * POOL_GROUP, sums[g] / cnts[g], pooled)
    return pooled - u


def _pool_prompt_kernel(u_ref, w_ref, scale_ref, o_ref, st_ref, ext_ref, lv_a, lv_b, *, tl):
    j = pl.program_id(1)
    base = _POOL_PAD + _POOL_CARRY
    n = base + tl
    zpad = jnp.zeros((_POOL_PAD, POOL_WIDTH), F32)
    ext_ref[0:_POOL_PAD, :] = zpad
    lv_a[0:_POOL_PAD, :] = zpad
    lv_b[0:_POOL_PAD, :] = zpad

    @pl.when(j == 0)
    def _():
        ext_ref[_POOL_PAD:base, :] = jnp.zeros((_POOL_CARRY, POOL_WIDTH), F32)

    u = u_ref[...]
    ext_ref[base:n, :] = u
    lo = _POOL_PAD
    lv_a[lo:n, :] = ext_ref[lo:n, :] + ext_ref[lo - 1:n - 1, :]
    s2 = lv_a[base:n, :]
    lv_b[lo:n, :] = lv_a[lo:n, :] + lv_a[lo - 2:n - 2, :]
    s4 = lv_b[base:n, :]
    lv_a[lo:n, :] = lv_b[lo:n, :] + lv_b[lo - 4:n - 4, :]
    s8 = lv_a[base:n, :]
    s16 = lv_a[base:n, :] + lv_a[base - 8:n - 8, :]
    pos1 = (lax.broadcasted_iota(jnp.int32, (tl, POOL_WIDTH), 0) + j * tl + 1).astype(F32)
    cnts = [jnp.minimum(pos1, float(w)) for w in POOL_WINDOWS]
    diff = _pool_select([s2, s4, s8, s16], cnts, u)
    o_ref[...] = _dot(diff.astype(BF16), w_ref[...]) * scale_ref[...]
    last = ext_ref[n - _POOL_CARRY:n, :]
    st_ref[0] = last
    ext_ref[_POOL_PAD:base, :] = last


def _pool_prompt(u, wbd, scale, batch, seq):
    tl = 512
    nj = seq // tl
    kern = functools.partial(_pool_prompt_kernel, tl=tl)
    rows = _POOL_PAD + _POOL_CARRY + tl
    return pl.pallas_call(
        kern,
        grid=(batch, nj),
        in_specs=[pl.BlockSpec((tl, POOL_WIDTH), lambda b, j: (b * nj + j, 0)),
                  pl.BlockSpec((POOL_WIDTH, POOL_WIDTH), lambda b, j: (0, 0)),
                  pl.BlockSpec((1, POOL_WIDTH), lambda b, j: (0, 0))],
        out_specs=[pl.BlockSpec((tl, POOL_WIDTH), lambda b, j: (b * nj + j, 0)),
                   pl.BlockSpec((1, _POOL_CARRY, POOL_WIDTH), lambda b, j: (b, 0, 0))],
        out_shape=[jax.ShapeDtypeStruct((batch * seq, POOL_WIDTH), F32),
                   jax.ShapeDtypeStruct((batch, _POOL_CARRY, POOL_WIDTH), F32)],
        scratch_shapes=[pltpu.VMEM((rows, POOL_WIDTH), F32)] * 3,
        compiler_params=_params(("arbitrary", "arbitrary")),
    )(u, wbd, scale)


def _pool_sample_kernel(past_ref, u_ref, w_ref, scale_ref, o_ref, *, n_past, steps, start):
    ext = [past_ref[i] for i in range(n_past)] + [u_ref[t] for t in range(steps)]
    for t in range(steps):
        idx = n_past + t
        sums, cnts = [], []
        acc = ext[idx]
        taken = 1
        for w in POOL_WINDOWS:
            while taken < w:
                if idx - taken >= 0:
                    acc = acc + ext[idx - taken]
                taken += 1
            sums.append(acc)
            cnts.append(float(min(start + t + 1, w)))
        diff = _pool_select(sums, cnts, ext[idx])
        o_ref[t] = _dot(diff.astype(BF16), w_ref[...]) * scale_ref[...]


def _pool_sample(past_t, u_t, wbd, scale, start):
    n_past, bs, _ = past_t.shape
    steps = u_t.shape[0]
    kern = functools.partial(_pool_sample_kernel, n_past=n_past, steps=steps, start=start)
    return pl.pallas_call(
        kern,
        out_shape=jax.ShapeDtypeStruct((steps, bs, POOL_WIDTH), F32),
        compiler_params=pltpu.CompilerParams(vmem_limit_bytes=VMEM_LIMIT),
    )(past_t, u_t, wbd, scale)


def _conv_sample_kernel(past_ref, x_ref, w_ref, b_ref, o_ref, *, steps):
    ext = [past_ref[i] for i in range(CONV_W - 1)] + [x_ref[t] for t in range(steps)]
    for t in range(steps):
        acc = b_ref[...]
        for j in range(CONV_W):
            acc = acc + ext[t + j] * w_ref[j:j + 1, :]
        o_ref[t] = _silu(acc)


def _conv_sample(past_t, x_t, cw, cb):
    steps, bs, _ = x_t.shape
    return pl.pallas_call(
        functools.partial(_conv_sample_kernel, steps=steps),
        out_shape=jax.ShapeDtypeStruct((steps, bs, CONV_DIM), F32),
        compiler_params=pltpu.CompilerParams(vmem_limit_bytes=VMEM_LIMIT),
    )(past_t, x_t, cw, cb)


def _softplus(x):
    return jnp.maximum(x, 0.0) + jnp.log(1.0 + jnp.exp(-jnp.abs(x)))


def _ssd_prepare(xc, dt_raw, dtb, alog, seq_len):
    q = xc.shape[0]
    dt = _softplus(dt_raw + dtb)
    dta = dt * (-jnp.exp(alog))
    ti = lax.broadcasted_iota(jnp.int32, (q, q), 0)
    si = lax.broadcasted_iota(jnp.int32, (q, q), 1)
    same = (ti // seq_len) == (si // seq_len)
    causal = same & (si <= ti)
    after = same & (si > ti)
    acum = _dot_exact(causal.astype(F32), dta)
    suf = _dot_exact(after.astype(F32), dta)
    return dt.T, acum, acum.T, suf.T, causal


def _ssd_diag_head(h, xc, cb, causal, acum, acum_t, dt_t):
    col = acum[:, h:h + 1]
    row = acum_t[h:h + 1, :]
    decay = jnp.where(causal, jnp.exp(jnp.minimum(col - row, 0.0)), 0.0)
    mat = cb * decay * dt_t[h:h + 1, :]
    xh = xc[:, h * HEAD_DIM:(h + 1) * HEAD_DIM]
    return _dot(mat.astype(BF16), xh.astype(BF16)), xh


def _ssd_gate_norm(y, z, ng):
    return _rms(y * _silu(z), ng)


def _ssd_prompt_kernel(xbc_ref, dt_ref, z_ref, cw_ref, cb_ref, dtb_ref, alog_ref, dskip_ref, ng_ref,
                       y_ref, hout_ref, ext_ref, h_ref, ybuf_ref):
    c = pl.program_id(1)
    q = SSD_CHUNK

    @pl.when(c == 0)
    def _():
        ext_ref[0:SUBLANES, :] = jnp.zeros((SUBLANES, CONV_DIM), F32)
        h_ref[...] = jnp.zeros_like(h_ref)

    ext_ref[SUBLANES:SUBLANES + q, :] = xbc_ref[...]
    acc = cb_ref[...]
    for j in range(CONV_W):
        lo = SUBLANES - (CONV_W - 1) + j
        acc = acc + ext_ref[lo:lo + q, :] * cw_ref[j:j + 1, :]
    xc = _silu(acc)
    ext_ref[0:SUBLANES, :] = ext_ref[q:q + SUBLANES, :]

    dt_t, acum, acum_t, suf_t, causal = _ssd_prepare(xc, dt_ref[...], dtb_ref[...], alog_ref[...], q)
    x_t = xc[:, 0:SSD_WIDTH].T
    for g in range(SSD_GROUPS):
        bg = xc[:, SSD_WIDTH + g * D_STATE:SSD_WIDTH + (g + 1) * D_STATE].astype(BF16)
        cg = xc[:, SSD_WIDTH + (SSD_GROUPS + g) * D_STATE:SSD_WIDTH + (SSD_GROUPS + g + 1) * D_STATE].astype(BF16)
        cb = _dot_nt(cg, bg)
        for r in range(HEADS_PER_GROUP):
            h = g * HEADS_PER_GROUP + r
            yd, xh = _ssd_diag_head(h, xc, cb, causal, acum, acum_t, dt_t)
            hst = h_ref[h]
            yo = _dot_nt(cg, hst.astype(BF16)) * jnp.exp(acum[:, h:h + 1])
            ybuf_ref[:, h * HEAD_DIM:(h + 1) * HEAD_DIM] = yd + yo + dskip_ref[0:1, h:h + 1] * xh
            wrow = jnp.exp(suf_t[h:h + 1, :]) * dt_t[h:h + 1, :]
            st = _dot((x_t[h * HEAD_DIM:(h + 1) * HEAD_DIM, :] * wrow).astype(BF16), bg)
            total = acum_t[h:h + 1, q - 1:q]
            h_ref[h] = jnp.exp(total) * hst + st
    y_ref[...] = _ssd_gate_norm(ybuf_ref[...], z_ref[...], ng_ref[...])
    hout_ref[0] = h_ref[...]


def _ssd_prompt(xbc, dt, z, cw, cb, dtb, alog, dskip, ng, batch, seq):
    q = SSD_CHUNK
    nc = seq // q
    tok = lambda b, c: (b * nc + c, 0)
    const = lambda b, c: (0, 0)
    return pl.pallas_call(
        _ssd_prompt_kernel,
        grid=(batch, nc),
        in_specs=[pl.BlockSpec((q, CONV_DIM), tok),
                  pl.BlockSpec((q, LANES), tok),
                  pl.BlockSpec((q, SSD_WIDTH), tok),
                  pl.BlockSpec((CONV_W, CONV_DIM), const),
                  pl.BlockSpec((1, CONV_DIM), const),
                  pl.BlockSpec((1, LANES), const),
                  pl.BlockSpec((1, LANES), const),
                  pl.BlockSpec((1, LANES), const),
                  pl.BlockSpec((1, SSD_WIDTH), const)],
        out_specs=[pl.BlockSpec((q, SSD_WIDTH), tok),
                   pl.BlockSpec((1, SSD_HEADS, HEAD_DIM, D_STATE), lambda b, c: (b, 0, 0, 0))],
        out_shape=[jax.ShapeDtypeStruct((batch * seq, SSD_WIDTH), F32),
                   jax.ShapeDtypeStruct((batch, SSD_HEADS, HEAD_DIM, D_STATE), F32)],
        scratch_shapes=[pltpu.VMEM((q + 2 * SUBLANES, CONV_DIM), F32),
                        pltpu.VMEM((SSD_HEADS, HEAD_DIM, D_STATE), F32),
                        pltpu.VMEM((q, SSD_WIDTH), F32)],
        compiler_params=_params(("arbitrary", "arbitrary")),
    )(xbc, dt, z, cw, cb, dtb, alog, dskip, ng)


def _ssd_sample_kernel(xc_ref, dt_ref, z_ref, h0_ref, dtb_ref, alog_ref, dskip_ref, ng_ref,
                       y_ref, hout_ref, ybuf_ref, acum_ref, xw_ref, *, steps):
    rows = xc_ref.shape[0]
    xc = xc_ref[...]
    dt_t, acum, acum_t, suf_t, causal = _ssd_prepare(xc, dt_ref[...], dtb_ref[...], alog_ref[...], steps)
    acum_ref[...] = acum
    x_t = xc[:, 0:SSD_WIDTH].T
    for g in range(SSD_GROUPS):
        bg = xc[:, SSD_WIDTH + g * D_STATE:SSD_WIDTH + (g + 1) * D_STATE].astype(BF16)
        cg = xc[:, SSD_WIDTH + (SSD_GROUPS + g) * D_STATE:SSD_WIDTH + (SSD_GROUPS + g + 1) * D_STATE].astype(BF16)
        cb = _dot_nt(cg, bg)
        for r in range(HEADS_PER_GROUP):
            h = g * HEADS_PER_GROUP + r
            yd, xh = _ssd_diag_head(h, xc, cb, causal, acum, acum_t, dt_t)
            ybuf_ref[:, h * HEAD_DIM:(h + 1) * HEAD_DIM] = yd + dskip_ref[0:1, h:h + 1] * xh
            wrow = jnp.exp(suf_t[h:h + 1, :]) * dt_t[h:h + 1, :]
            xw_ref[h * HEAD_DIM:(h + 1) * HEAD_DIM, :] = x_t[h * HEAD_DIM:(h + 1) * HEAD_DIM, :] * wrow

    lane = lax.broadcasted_iota(jnp.int32, (1, rows), 1)

    def per_seq(b, carry):
        r0 = pl.multiple_of(b * steps, steps)
        in_seq = (lane // steps) == b
        ac = acum_ref[pl.ds(r0, steps), :]
        for h in range(SSD_HEADS):
            g = h // HEADS_PER_GROUP
            c_lo = SSD_WIDTH + (SSD_GROUPS + g) * D_STATE
            b_lo = SSD_WIDTH + g * D_STATE
            hst = h0_ref[b, h]
            cg = xc_ref[pl.ds(r0, steps), c_lo:c_lo + D_STATE].astype(BF16)
            yo = _dot_nt(cg, hst.astype(BF16)) * jnp.exp(ac[:, h:h + 1])
            sl = slice(h * HEAD_DIM, (h + 1) * HEAD_DIM)
            ybuf_ref[pl.ds(r0, steps), sl] = ybuf_ref[pl.ds(r0, steps), sl] + yo
            xw = jnp.where(in_seq, xw_ref[sl, :], 0.0).astype(BF16)
            st = _dot(xw, xc_ref[:, b_lo:b_lo + D_STATE].astype(BF16))
            total = ac[steps - 1:steps, h:h + 1]
            hout_ref[b, h] = jnp.exp(total) * hst + st
        return carry

    lax.fori_loop(0, rows // steps, per_seq, 0)
    y_ref[...] = _ssd_gate_norm(ybuf_ref[...], z_ref[...], ng_ref[...])


def _ssd_sample(xc, dt, z, h0, dtb, alog, dskip, ng, row0, steps):
    nseq = h0.shape[0]
    nb = SEQS_PER_SSD_BLOCK
    rows = nb * steps
    blk0 = row0 // rows
    const = lambda i: (0, 0)
    return pl.pallas_call(
        functools.partial(_ssd_sample_kernel, steps=steps),
        grid=(nseq // nb,),
        in_specs=[pl.BlockSpec((rows, CONV_DIM), lambda i: (i, 0)),
                  pl.BlockSpec((rows, LANES), lambda i: (blk0 + i, 0)),
                  pl.BlockSpec((rows, SSD_WIDTH), lambda i: (blk0 + i, 0)),
                  pl.BlockSpec((nb, SSD_HEADS, HEAD_DIM, D_STATE), lambda i: (i, 0, 0, 0)),
                  pl.BlockSpec((1, LANES), const),
                  pl.BlockSpec((1, LANES), const),
                  pl.BlockSpec((1, LANES), const),
                  pl.BlockSpec((1, SSD_WIDTH), const)],
        out_specs=[pl.BlockSpec((rows, SSD_WIDTH), lambda i: (i, 0)),
                   pl.BlockSpec((nb, SSD_HEADS, HEAD_DIM, D_STATE), lambda i: (i, 0, 0, 0))],
        out_shape=[jax.ShapeDtypeStruct((nseq * steps, SSD_WIDTH), F32),
                   jax.ShapeDtypeStruct((nseq, SSD_HEADS, HEAD_DIM, D_STATE), F32)],
        scratch_shapes=[pltpu.VMEM((rows, SSD_WIDTH), F32),
                        pltpu.VMEM((rows, LANES), F32),
                        pltpu.VMEM((SSD_WIDTH, rows), F32)],
        compiler_params=_params(("arbitrary",)),
    )(xc, dt, z, h0, dtb, alog, dskip, ng)


def _attn_prompt_kernel(q_ref, kc_ref, kp_ref, vc_ref, vp_ref, o_ref, m_ref, l_ref):
    jb = pl.program_id(2)
    blk = ATT_BLOCK
    q = (q_ref[...] * (HEAD_DIM ** -0.5)).astype(BF16)
    kc = kc_ref[...].astype(BF16)
    kp = kp_ref[...].astype(BF16)
    vc = vc_ref[...].astype(BF16)
    vp = vp_ref[...].astype(BF16)
    qi = lax.broadcasted_iota(jnp.int32, (blk, blk), 0)
    kj = lax.broadcasted_iota(jnp.int32, (blk, blk), 1)
    mask_c = kj <= qi
    mask_p = (kj >= qi) & (jb > 0)
    for h in range(ATT_HEADS):
        sl = slice(h * HEAD_DIM, (h + 1) * HEAD_DIM)
        sc = jnp.where(mask_c, _dot_nt(q[:, sl], kc[:, sl]), NEG)
        sp = jnp.where(mask_p, _dot_nt(q[:, sl], kp[:, sl]), NEG)
        m = jnp.maximum(jnp.max(sc, axis=-1, keepdims=True), jnp.max(sp, axis=-1, keepdims=True))
        pc = jnp.exp(sc - m)
        pp = jnp.exp(sp - m)
        l = jnp.sum(pc, axis=-1, keepdims=True) + jnp.sum(pp, axis=-1, keepdims=True)
        o_ref[:, sl] = _dot(pc.astype(BF16), vc[:, sl]) + _dot(pp.astype(BF16), vp[:, sl])
        m_ref[:, sl] = jnp.broadcast_to(m, (blk, HEAD_DIM))
        l_ref[:, sl] = jnp.broadcast_to(l, (blk, HEAD_DIM))


def _attn_prompt(q, k, v, batch, seq, d):
    t = q.shape[0]
    blk = ATT_BLOCK
    nj = seq // d // blk
    view = lambda a: a.reshape(t // d, d * ATT_WIDTH)
    cur = lambda b, r, j: (b * nj + j, r)
    prev = lambda b, r, j: (b * nj + jnp.maximum(j - 1, 0), r)
    spec = lambda m: pl.BlockSpec((blk, ATT_WIDTH), m)
    out = jax.ShapeDtypeStruct((batch * seq // d, d * ATT_WIDTH), F32)
    o, m, l = pl.pallas_call(
        _attn_prompt_kernel,
        grid=(batch, d, nj),
        in_specs=[spec(cur), spec(cur), spec(prev), spec(cur), spec(prev)],
        out_specs=[spec(cur)] * 3,
        out_shape=[out] * 3,
        compiler_params=_params(("arbitrary", "arbitrary", "arbitrary")),
    )(view(q), view(k), view(k), view(v), view(v))
    flat = lambda a: a.reshape(batch * seq, ATT_WIDTH)
    return flat(o), flat(m), flat(l)


def _attn_merge_kernel(*refs):
    out_ref = refs[-1]
    n = (len(refs) - 1) // 3
    ms = [refs[3 * i + 1][...] for i in range(n)]
    m = ms[0]
    for mi in ms[1:]:
        m = jnp.maximum(m, mi)
    num = 0.0
    den = 0.0
    for i in range(n):
        w = jnp.exp(ms[i] - m)
        num = num + w * refs[3 * i][...]
        den = den + w * refs[3 * i + 2][...]
    out_ref[...] = num / den


def _attn_merge(parts):
    rows = parts[0][0].shape[0]
    tm = 512
    spec = pl.BlockSpec((tm, ATT_WIDTH), lambda i: (i, 0))
    flat = [a for p in parts for a in p]
    return pl.pallas_call(
        _attn_merge_kernel,
        grid=(rows // tm,),
        in_specs=[spec] * len(flat),
        out_specs=spec,
        out_shape=jax.ShapeDtypeStruct((rows, ATT_WIDTH), F32),
        compiler_params=_params(("arbitrary",)),
    )(*flat)


def _sample_multiplicity(steps, wbuf):
    qpos = wbuf + np.arange(steps)[:, None]
    kpos = np.arange(wbuf + steps)[None, :]
    delta = qpos - kpos
    cnt = np.zeros(delta.shape, np.float32)
    for w, d in DILATED:
        cnt += ((delta >= 0) & (delta % d == 0) & (delta // d <= w // d)).astype(np.float32)
    return cnt


def _attn_sample(q, k, v, cache_k, cache_v, layer, row0, steps, prev_out):
    depth, nseq, wbuf = cache_k.shape[0], cache_k.shape[1], cache_k.shape[2]
    ck = cache_k.reshape(depth, nseq, wbuf, ATT_WIDTH)
    cv = cache_v.reshape(depth, nseq, wbuf, ATT_WIDTH)
    t = q.shape[0]
    view = lambda a: a.reshape(t // steps, steps, ATT_WIDTH)
    b0 = row0 // steps
    new = pl.BlockSpec((1, steps, ATT_WIDTH), lambda b: (b0 + b, 0, 0))
    window = pl.BlockSpec((1, 1, wbuf, ATT_WIDTH), lambda b: (layer, b, 0, 0))
    cnt = np.tile(_sample_multiplicity(steps, wbuf), (ATT_HEADS, 1))
    cntc = jnp.asarray(cnt[:, :wbuf])
    cntn = jnp.asarray(np.pad(cnt[:, wbuf:], ((0, 0), (0, LANES - steps))))
    rows = ATT_HEADS * steps
    in_specs = [new, new, new, window, window,
                pl.BlockSpec((rows, wbuf), lambda b: (0, 0)),
                pl.BlockSpec((rows, LANES), lambda b: (0, 0))]
    args = [view(q), view(k), view(v), ck, cv, cntc, cntn]
    aliases = {}
    if prev_out is not None:
        in_specs += [pl.BlockSpec(memory_space=pl.ANY)] * 2
        args += list(prev_out)
        aliases = {len(args) - 2: 1, len(args) - 1: 2}
    stacked = jax.ShapeDtypeStruct((depth, nseq, wbuf, ATT_WIDTH), F32)
    return pl.pallas_call(
        functools.partial(_attn_sample_kernel, steps=steps, wbuf=wbuf),
        grid=(nseq,),
        in_specs=in_specs,
        out_specs=[pl.BlockSpec((1, steps, ATT_WIDTH), lambda b: (b, 0, 0)), window, window],
        out_shape=[jax.ShapeDtypeStruct((nseq, steps, ATT_WIDTH), F32), stacked, stacked],
        input_output_aliases=aliases,
        compiler_params=_params(("arbitrary",)),
    )(*args)


def _attn_sample_kernel(*refs, steps, wbuf):
    q_ref, kn_ref, vn_ref, kc_ref, vc_ref, cntc_ref, cntn_ref = refs[:7]
    att_ref, ko_ref, vo_ref = refs[-3:]
    rows = ATT_HEADS * steps
    q = q_ref[0] * (HEAD_DIM ** -0.5)
    q6 = jnp.concatenate([q] * ATT_HEADS, axis=0)
    row = lax.broadcasted_iota(jnp.int32, (rows, ATT_WIDTH), 0)
    lane = lax.broadcasted_iota(jnp.int32, (rows, ATT_WIDTH), 1)
    own = (lane // HEAD_DIM) == (row // steps)
    qbd = jnp.where(own, q6, 0.0).astype(BF16)
    kc = kc_ref[0, 0]
    vc = vc_ref[0, 0]
    kn = kn_ref[0]
    vn = vn_ref[0]
    zpad = jnp.zeros((LANES - steps, ATT_WIDTH), F32)
    knp = jnp.concatenate([kn, zpad], axis=0).astype(BF16)
    vnp = jnp.concatenate([vn, zpad], axis=0).astype(BF16)
    cntc = cntc_ref[...]
    cntn = cntn_ref[...]
    s_c = jnp.where(cntc > 0.0, _dot_nt(qbd, kc.astype(BF16)), NEG)
    s_n = jnp.where(cntn > 0.0, _dot_nt(qbd, knp), NEG)
    m = jnp.maximum(jnp.max(s_c, axis=-1, keepdims=True), jnp.max(s_n, axis=-1, keepdims=True))
    p_c = cntc * jnp.exp(s_c - m)
    p_n = cntn * jnp.exp(s_n - m)
    l = jnp.sum(p_c, axis=-1, keepdims=True) + jnp.sum(p_n, axis=-1, keepdims=True)
    o = _dot(p_c.astype(BF16), vc.astype(BF16)) + _dot(p_n.astype(BF16), vnp)
    o = jnp.where(own, o / l, 0.0)
    att_ref[0] = jnp.sum(o.reshape(ATT_HEADS, steps, ATT_WIDTH), axis=0)
    ko_ref[0, 0, 0:wbuf - steps, :] = kc[steps:wbuf, :]
    ko_ref[0, 0, wbuf - steps:wbuf, :] = kn
    vo_ref[0, 0, 0:wbuf - steps, :] = vc[steps:wbuf, :]
    vo_ref[0, 0, wbuf - steps:wbuf, :] = vn


def _router_kernel(lg_ref, sel_ref, gate_ref, cnt_ref, run_ref):
    i = pl.program_id(0)

    @pl.when(i == 0)
    def _():
        run_ref[...] = jnp.zeros_like(run_ref)

    tm = lg_ref.shape[0]
    work = lg_ref[...]
    lane = lax.broadcasted_iota(jnp.int32, (tm, LANES), 1)
    vals, hots = [], []
    for _ in range(TOP_K):
        m = jnp.max(work, axis=-1, keepdims=True)
        idx = jnp.min(jnp.where(work == m, lane, LANES), axis=-1, keepdims=True)
        hot = lane == idx
        vals.append(m)
        hots.append(hot)
        work = jnp.where(hot, -jnp.inf, work)
    exps = [jnp.exp(v - vals[0]) for v in vals]
    den = exps[0]
    for e in exps[1:]:
        den = den + e
    chosen = hots[0]
    for hot in hots[1:]:
        chosen = chosen | hot
    onehot = chosen.astype(F32)
    ri = lax.broadcasted_iota(jnp.int32, (tm, tm), 0)
    ci = lax.broadcasted_iota(jnp.int32, (tm, tm), 1)
    before = (ci < ri).astype(BF16)
    rank = _dot(before, onehot.astype(BF16)) + run_ref[...]
    sel = jnp.zeros((tm, LANES), jnp.int32)
    gates = jnp.zeros((tm, LANES), F32)
    for k in range(TOP_K):
        e_k = jnp.min(jnp.where(hots[k], lane, LANES), axis=-1, keepdims=True)
        r_k = jnp.sum(jnp.where(hots[k], rank, 0.0), axis=-1, keepdims=True).astype(jnp.int32)
        sel = jnp.where(lane == k, e_k, sel)
        sel = jnp.where(lane == TOP_K + k, r_k, sel)
        gates = jnp.where(lane == k, exps[k] / den, gates)
    sel_ref[...] = sel
    gate_ref[...] = gates
    run_ref[...] = run_ref[...] + jnp.sum(onehot, axis=0, keepdims=True)
    cnt_ref[...] = run_ref[...]


def _router(logits):
    t = logits.shape[0]
    tm = TOKEN_TILE
    tok = lambda i: (i, 0)
    return pl.pallas_call(
        _router_kernel,
        grid=(t // tm,),
        in_specs=[pl.BlockSpec((tm, LANES), tok)],
        out_specs=[pl.BlockSpec((tm, LANES), tok), pl.BlockSpec((tm, LANES), tok),
                   pl.BlockSpec((1, LANES), lambda i: (0, 0))],
        out_shape=[jax.ShapeDtypeStruct((t, LANES), jnp.int32), jax.ShapeDtypeStruct((t, LANES), F32),
                   jax.ShapeDtypeStruct((1, LANES), F32)],
        scratch_shapes=[pltpu.VMEM((1, LANES), F32)],
        compiler_params=_params(("arbitrary",)),
    )(logits)


def _row_copy(src, dst, sem):
    return pltpu.make_async_copy(src, dst, sem)


def _dispatch_kernel(dest_ref, h2_ref, xg_in_ref, xg_ref, sem):
    del xg_in_ref
    i = pl.program_id(0)
    tm = h2_ref.shape[0]

    def issue(r, carry):
        for k in range(TOP_K):
            d = dest_ref[(i * tm + r) * TOP_K + k]
            _row_copy(h2_ref.at[pl.ds(r, 1)], xg_ref.at[pl.ds(d, 1)], sem).start()
        return carry

    lax.fori_loop(0, tm, issue, 0)

    def drain(r, carry):
        for k in range(TOP_K):
            _row_copy(h2_ref.at[pl.ds(0, 1)], xg_ref.at[pl.ds(0, 1)], sem).wait()
        return carry

    lax.fori_loop(0, tm, drain, 0)


def _dispatch(dest_flat, h2, n_rows):
    t = h2.shape[0]
    tm = TOKEN_TILE
    xg0 = jnp.zeros((n_rows, D_MODEL), F32)
    return pl.pallas_call(
        _dispatch_kernel,
        grid_spec=pltpu.PrefetchScalarGridSpec(
            num_scalar_prefetch=1,
            grid=(t // tm,),
            in_specs=[pl.BlockSpec((tm, D_MODEL), lambda i, d: (i, 0)),
                      pl.BlockSpec(memory_space=pl.ANY)],
            out_specs=pl.BlockSpec(memory_space=pl.ANY),
            scratch_shapes=[pltpu.SemaphoreType.DMA(())]),
        out_shape=jax.ShapeDtypeStruct((n_rows, D_MODEL), F32),
        input_output_aliases={2: 0},
        compiler_params=_params(("arbitrary",)),
    )(dest_flat, h2, xg0)


def _expert_kernel(be_ref, nu_ref, x_ref, wgu_ref, bgu_ref, wd_ref, bd_ref, y_ref):
    i = pl.program_id(0)

    @pl.when(i < nu_ref[0])
    def _():
        gu = _dot(x_ref[...].astype(BF16), wgu_ref[0]) + bgu_ref[0]
        gate = jnp.minimum(gu[:, :D_FF], SWIGLU_LIMIT)
        up = jnp.clip(gu[:, D_FF:], -SWIGLU_LIMIT, SWIGLU_LIMIT)
        act = (up + 1.0) * gate * _sigmoid(SWIGLU_ALPHA * gate)
        y_ref[...] = _dot(act.astype(BF16), wd_ref[0]) + bd_ref[0]

    @pl.when(i >= nu_ref[0])
    def _():
        y_ref[...] = jnp.zeros_like(y_ref)


def _experts(block_e, n_used, xg, wgu, bgu, wd, bd):
    n_rows = xg.shape[0]
    tb = EXPERT_BLOCK
    blk = lambda i, be, nu: (jnp.minimum(i, nu[0] - 1), 0)
    exp3 = lambda i, be, nu: (be[i], 0, 0)
    return pl.pallas_call(
        _expert_kernel,
        grid_spec=pltpu.PrefetchScalarGridSpec(
            num_scalar_prefetch=2,
            grid=(n_rows // tb,),
            in_specs=[pl.BlockSpec((tb, D_MODEL), blk),
                      pl.BlockSpec((1, D_MODEL, 2 * D_FF), exp3),
                      pl.BlockSpec((1, 1, 2 * D_FF), exp3),
                      pl.BlockSpec((1, D_FF, D_MODEL), exp3),
                      pl.BlockSpec((1, 1, D_MODEL), exp3)],
            out_specs=pl.BlockSpec((tb, D_MODEL), lambda i, be, nu: (i, 0))),
        out_shape=jax.ShapeDtypeStruct((n_rows, D_MODEL), F32),
        compiler_params=_params(("arbitrary",)),
    )(block_e, n_used, xg, wgu, bgu, wd, bd)


def _combine_kernel(dest_ref, yg_ref, x1_ref, gate_ref, gt_ref, o_ref, buf_ref, sem):
    i = pl.program_id(0)
    tm = x1_ref.shape[0]

    def issue(r, carry):
        for k in range(TOP_K):
            d = dest_ref[(i * tm + r) * TOP_K + k]
            _row_copy(yg_ref.at[pl.ds(d, 1)], buf_ref.at[k, pl.ds(r, 1)], sem).start()
        return carry

    lax.fori_loop(0, tm, issue, 0)

    def drain(r, carry):
        for k in range(TOP_K):
            _row_copy(yg_ref.at[pl.ds(0, 1)], buf_ref.at[k, pl.ds(0, 1)], sem).wait()
        return carry

    lax.fori_loop(0, tm, drain, 0)
    gates = gate_ref[...]
    ff = gates[:, 0:1] * buf_ref[0]
    for k in range(1, TOP_K):
        ff = ff + gates[:, k:k + 1] * buf_ref[k]
    o_ref[...] = x1_ref[...] + gt_ref[...] * ff


def _combine(dest_flat, yg, x1, gates, gtx, mod_map):
    t = x1.shape[0]
    tm = TOKEN_TILE
    tok = lambda i, d: (i, 0)
    return pl.pallas_call(
        _combine_kernel,
        grid_spec=pltpu.PrefetchScalarGridSpec(
            num_scalar_prefetch=1,
            grid=(t // tm,),
            in_specs=[pl.BlockSpec(memory_space=pl.ANY),
                      pl.BlockSpec((tm, D_MODEL), tok),
                      pl.BlockSpec((tm, LANES), tok),
                      pl.BlockSpec((tm, D_MODEL), lambda i, d: mod_map(i))],
            out_specs=pl.BlockSpec((tm, D_MODEL), tok),
            scratch_shapes=[pltpu.VMEM((TOP_K, tm, D_MODEL), F32), pltpu.SemaphoreType.DMA(())]),
        out_shape=jax.ShapeDtypeStruct((t, D_MODEL), F32),
        compiler_params=_params(("arbitrary",)),
    )(dest_flat, yg, x1, gates, gtx)


def _route_tables(sel, counts, n_blocks):
    tb = EXPERT_BLOCK
    cnt = counts[0, :N_EXPERTS].astype(jnp.int32)
    padded = (cnt + tb - 1) // tb * tb
    ends = jnp.cumsum(padded)
    starts = ends - padded
    experts = sel[:, :TOP_K]
    ranks = sel[:, TOP_K:2 * TOP_K]
    dest = (starts[experts] + ranks).reshape(-1)
    block_e = jnp.clip(jnp.searchsorted(ends, jnp.arange(n_blocks, dtype=jnp.int32) * tb, side="right"),
                       0, N_EXPERTS - 1).astype(jnp.int32)
    n_used = (ends[-1:] // tb).astype(jnp.int32)
    return dest.astype(jnp.int32), block_e, n_used


def _pad_lanes(v, fill=0.0):
    v = v.reshape(1, -1).astype(F32)
    return jnp.pad(v, ((0, 0), (0, LANES - v.shape[1])), constant_values=fill)


def _pack_w_in(w):
    cuts = np.cumsum([0, POOL_WIDTH, SSD_WIDTH, CONV_DIM, SSD_HEADS, ATT_WIDTH, ATT_WIDTH, ATT_WIDTH])
    pieces = []
    for i, (_, width) in enumerate(_IN_COLS):
        piece = w[:, cuts[i]:cuts[i + 1]]
        pieces.append(jnp.pad(piece, ((0, 0), (0, width - piece.shape[1]))))
    return jnp.concatenate(pieces, axis=1).astype(BF16)


def _block_diag(pool_w):
    g, c, _ = pool_w.shape
    out = jnp.zeros((g * c, g * c), F32)
    for i in range(g):
        out = out.at[i * c:(i + 1) * c, i * c:(i + 1) * c].set(pool_w[i])
    return out.astype(BF16)


def kernel(x_prompt, x_sample, state_pool, state_conv, state_ssm, cache_k, cache_v, c_prompt, c_sample, g_mix, w_ada, b_ada, w_in, pool_w, pool_scale, conv_w, conv_b, dt_bias, a_log, d_skip, ssd_norm_g, w_out, g_ffn, router_w, router_b, w_gate_up, b_gate_up, w_down, b_down, g_final):
    batch, seq, _ = x_prompt.shape
    nseq, steps, _ = x_sample.shape
    depth = w_in.shape[0]
    wbuf = cache_k.shape[2]
    tp = batch * seq
    ts = nseq * steps
    t = tp + ts
    tm = TOKEN_TILE
    assert steps == SUBLANES and seq % (tm * 2) == 0 and ts % tm == 0 and nseq % SEQS_PER_SSD_BLOCK == 0
    assert seq % (DILATED[-1][1] * ATT_BLOCK) == 0 and t % DILATED[-1][1] == 0

    c_all = jnp.concatenate([c_prompt, c_sample], axis=0)
    pad_rows = (-c_all.shape[0]) % SUBLANES
    c_all = jnp.pad(c_all, ((0, pad_rows), (0, 0)))
    mod = _ada(c_all, w_ada, b_ada)

    tiles_per_seq = seq // tm
    n_prompt_tiles = tp // tm

    def mod_map(i):
        return (jnp.where(i < n_prompt_tiles, i // tiles_per_seq, batch + i - n_prompt_tiles), 0)

    def expand(m):
        mp = jnp.repeat(m[:batch], tm, axis=0)
        ms = jnp.repeat(m[batch:batch + nseq], steps, axis=0)
        return jnp.concatenate([mp, ms], axis=0)

    x = jnp.concatenate([x_prompt.reshape(tp, D_MODEL), x_sample.reshape(ts, D_MODEL)], axis=0)

    n_blocks = -(-(t * TOP_K) // EXPERT_BLOCK) + N_EXPERTS
    n_rows = n_blocks * EXPERT_BLOCK

    pool_p, pool_s, conv_p, conv_s, ssm_p, ssm_s, k_p, v_p = [], [], [], [], [], [], [], []
    kv_s = None
    for l in range(depth):
        sh1, sc1, gt1, sh2, sc2, gt2 = [expand(mod[l, :, i * D_MODEL:(i + 1) * D_MODEL]) for i in range(6)]
        u, z, xbc, dt, q, k, v = _inproj(x, sc1, sh1, g_mix[l].reshape(1, -1), _pack_w_in(w_in[l]), mod_map)

        wbd = _block_diag(pool_w[l])
        pscale = pool_scale[l].reshape(1, -1)
        cw, cb = conv_w[l], conv_b[l].reshape(1, -1)
        dtb, alog, dskip = _pad_lanes(dt_bias[l]), _pad_lanes(a_log[l]), _pad_lanes(d_skip[l])
        ng = ssd_norm_g[l].reshape(1, -1)

        pool_out_p, pool_last = _pool_prompt(u, wbd, pscale, batch, seq)
        ssd_out_p, ssm_new_p = _ssd_prompt(xbc, dt, z, cw, cb, dtb, alog, dskip, ng, batch, seq)
        att_p = _attn_merge([_attn_prompt(q, k, v, batch, seq, d) for _, d in DILATED])

        u_s = u[tp:].reshape(nseq, steps, POOL_WIDTH)
        xbc_s = xbc[tp:].reshape(nseq, steps, CONV_DIM)
        tl = lambda a: jnp.swapaxes(a, 0, 1)
        pool_out_s = tl(_pool_sample(tl(state_pool[l]), tl(u_s), wbd, pscale, wbuf)).reshape(ts, POOL_WIDTH)
        xc_s = tl(_conv_sample(tl(state_conv[l]), tl(xbc_s), cw, cb)).reshape(ts, CONV_DIM)
        ssd_out_s, ssm_new_s = _ssd_sample(xc_s, dt, z, state_ssm[l], dtb, alog, dskip, ng, tp, steps)
        att_s, ks_new, vs_new = _attn_sample(q, k, v, cache_k, cache_v, l, tp, steps, kv_s)
        kv_s = (ks_new, vs_new)

        pool_all = jnp.concatenate([pool_out_p, pool_out_s], axis=0)
        ssd_all = jnp.concatenate([ssd_out_p, ssd_out_s], axis=0)
        att_all = jnp.concatenate([att_p, att_s.reshape(ts, ATT_WIDTH)], axis=0)

        rw = jnp.pad(router_w[l], ((0, 0), (0, LANES - N_EXPERTS)))
        rb = _pad_lanes(router_b[l], fill=NEG)
        x1, h2, logits = _mixout(pool_all, ssd_all, att_all, x, gt1, sc2, sh2, g_ffn[l].reshape(1, -1),
                                 w_out[l].astype(BF16), rw, rb, mod_map)
        sel, gates, counts = _router(logits)
        dest, block_e, n_used = _route_tables(sel, counts, n_blocks)
        xg = _dispatch(dest, h2, n_rows)
        yg = _experts(block_e, n_used, xg, w_gate_up[l].astype(BF16), b_gate_up[l].reshape(N_EXPERTS, 1, -1),
                      w_down[l].astype(BF16), b_down[l].reshape(N_EXPERTS, 1, -1))
        x = _combine(dest, yg, x1, gates, gt2, mod_map)

        pool_p.append(pool_last[:, _POOL_CARRY - POOL_KEEP:])
        pool_s.append(jnp.concatenate([state_pool[l], u_s], axis=1)[:, -POOL_KEEP:])
        conv_p.append(xbc[:tp].reshape(batch, seq, CONV_DIM)[:, -(CONV_W - 1):])
        conv_s.append(jnp.concatenate([state_conv[l], xbc_s], axis=1)[:, -(CONV_W - 1):])
        ssm_p.append(ssm_new_p)
        ssm_s.append(ssm_new_s)
        keep = min(MAX_WINDOW, seq)
        k_p.append(k[:tp].reshape(batch, seq, ATT_HEADS, HEAD_DIM)[:, -keep:])
        v_p.append(v[:tp].reshape(batch, seq, ATT_HEADS, HEAD_DIM)[:, -keep:])

    y = _final_norm(x, g_final.reshape(1, -1))
    y_prompt = y[:tp].reshape(batch, seq, D_MODEL)
    y_sample = y[tp:].reshape(nseq, steps, D_MODEL)
    k_s = kv_s[0].reshape(depth, nseq, wbuf, ATT_HEADS, HEAD_DIM)
    v_s = kv_s[1].reshape(depth, nseq, wbuf, ATT_HEADS, HEAD_DIM)
    st = lambda xs: jnp.stack(xs, axis=0)
    return (y_prompt, y_sample, st(pool_p), st(pool_s), st(conv_p), st(conv_s), st(ssm_p), st(ssm_s),
            st(k_p), k_s, st(v_p), v_s)
```

```python
import functools

import numpy as np
import jax
import jax.numpy as jnp
from jax import lax
from jax.experimental import pallas as pl
from jax.experimental.pallas import tpu as pltpu

F32 = jnp.float32
BF16 = jnp.bfloat16

D_MODEL = 1024
HEAD_DIM = 64
POOL_WINDOWS = (2, 4, 8, 16)
POOL_WIDTH = 256
POOL_GROUP = 64
POOL_KEEP = 15
SSD_HEADS = 6
SSD_WIDTH = 384
SSD_GROUPS = 2
HEADS_PER_GROUP = SSD_HEADS // SSD_GROUPS
D_STATE = 64
CONV_W = 4
CONV_DIM = 640
SSD_CHUNK = 128
ATT_HEADS = 6
ATT_WIDTH = 384
DILATED = ((128, 1), (512, 4), (2048, 16))
MAX_WINDOW = 2048
ATT_BLOCK = 128
N_EXPERTS = 32
TOP_K = 4
D_FF = 1024
SWIGLU_LIMIT = 7.0
SWIGLU_ALPHA = 1.702
EPS = 1e-6

LANES = 128
SUBLANES = 8
TOKEN_TILE = 256
EXPERT_BLOCK = 256
SEQS_PER_SSD_BLOCK = 16
VMEM_LIMIT = 48 * 1024 * 1024
EXPERT_VMEM_LIMIT = 56 * 1024 * 1024
NEG = -1e30

_IN_COLS = (("u", POOL_WIDTH), ("z", SSD_WIDTH), ("xbc", CONV_DIM), ("dt", LANES),
            ("q", ATT_WIDTH), ("k", ATT_WIDTH), ("v", ATT_WIDTH))
IN_PACKED = sum(w for _, w in _IN_COLS)
_ATT_NAMES = ("q", "k", "v")
ATT_CHUNKS = ATT_WIDTH // LANES
ATT_TILE = DILATED[-1][1] * ATT_BLOCK


def _params(sem, vmem=VMEM_LIMIT):
    return pltpu.CompilerParams(dimension_semantics=sem, vmem_limit_bytes=vmem)


def _sigmoid(x):
    return 1.0 / (1.0 + jnp.exp(-x))


def _silu(x):
    return x * _sigmoid(x)


def _dot(a, b):
    return jnp.dot(a, b, preferred_element_type=F32)


def _dot_nt(a, b):
    return lax.dot_general(a, b, (((1,), (1,)), ((), ())), preferred_element_type=F32)


def _dot_exact(a, b):
    return jnp.dot(a, b, preferred_element_type=F32, precision=lax.Precision.HIGHEST)


def _rms(x, g):
    ms = jnp.mean(x * x, axis=-1, keepdims=True)
    return x * lax.rsqrt(ms + EPS) * g


def _ada_kernel(c_ref, w_ref, b_ref, o_ref):
    sc = _silu(c_ref[...]).astype(BF16)
    o_ref[0] = _dot(sc, w_ref[0].astype(BF16)) + b_ref[0]


def _ada(c_all, w_ada, b_ada):
    depth, _, n = w_ada.shape
    rows = c_all.shape[0]
    tn = 512
    return pl.pallas_call(
        _ada_kernel,
        grid=(depth, n // tn),
        in_specs=[pl.BlockSpec((rows, D_MODEL), lambda l, j: (0, 0)),
                  pl.BlockSpec((1, D_MODEL, tn), lambda l, j: (l, 0, j)),
                  pl.BlockSpec((1, 1, tn), lambda l, j: (l, 0, j))],
        out_specs=pl.BlockSpec((1, rows, tn), lambda l, j: (l, 0, j)),
        out_shape=jax.ShapeDtypeStruct((depth, rows, n), F32),
        compiler_params=_params(("arbitrary", "arbitrary")),
    )(c_all, w_ada, b_ada.reshape(depth, 1, n))


def _inproj_kernel(x_ref, sc_ref, sh_ref, g_ref, w_ref, u_ref, z_ref, xbc_ref, dt_ref, q_ref, k_ref, v_ref):
    h = _rms(x_ref[...], g_ref[...]) * (1.0 + sc_ref[...]) + sh_ref[...]
    r = _dot(h.astype(BF16), w_ref[...])
    off = 0
    for ref, (name, w) in zip((u_ref, z_ref, xbc_ref, dt_ref, q_ref, k_ref, v_ref), _IN_COLS):
        if name in _ATT_NAMES:
            for c in range(ATT_CHUNKS):
                ref[c] = r[:, off + c * LANES:off + (c + 1) * LANES]
        else:
            ref[...] = r[:, off:off + w]
        off += w


def _inproj(x, scx, shx, g, w_packed, mod_map):
    t = x.shape[0]
    tm = TOKEN_TILE
    tok = lambda i: (i, 0)
    const = lambda i: (0, 0)
    out_specs, out_shape = [], []
    for name, w in _IN_COLS:
        if name in _ATT_NAMES:
            out_specs.append(pl.BlockSpec((ATT_CHUNKS, tm, LANES), lambda i: (0, i, 0)))
            out_shape.append(jax.ShapeDtypeStruct((ATT_CHUNKS, t, LANES), F32))
        else:
            out_specs.append(pl.BlockSpec((tm, w), tok))
            out_shape.append(jax.ShapeDtypeStruct((t, w), F32))
    return pl.pallas_call(
        _inproj_kernel,
        grid=(t // tm,),
        in_specs=[pl.BlockSpec((tm, D_MODEL), tok),
                  pl.BlockSpec((tm, D_MODEL), mod_map),
                  pl.BlockSpec((tm, D_MODEL), mod_map),
                  pl.BlockSpec((1, D_MODEL), const),
                  pl.BlockSpec((D_MODEL, IN_PACKED), const)],
        out_specs=out_specs,
        out_shape=out_shape,
        compiler_params=_params(("arbitrary",)),
    )(x, scx, shx, g, w_packed)


def _mixout_kernel(pool_ref, ssd_ref, att_ref, x_ref, gt_ref, sc_ref, sh_ref, g_ref, wo_ref, rw_ref, rb_ref,
                   x1_ref, h2_ref, lg_ref):
    mix = _dot(pool_ref[...].astype(BF16), wo_ref[0:POOL_WIDTH, :])
    mix = mix + _dot(ssd_ref[...].astype(BF16), wo_ref[POOL_WIDTH:POOL_WIDTH + SSD_WIDTH, :])
    mix = mix + _dot(att_ref[...].astype(BF16), wo_ref[POOL_WIDTH + SSD_WIDTH:, :])
    x1 = x_ref[...] + gt_ref[...] * mix
    x1_ref[...] = x1
    h2 = _rms(x1, g_ref[...]) * (1.0 + sc_ref[...]) + sh_ref[...]
    h2_ref[...] = h2
    lg_ref[...] = _dot_exact(h2, rw_ref[...]) + rb_ref[...]


def _mixout(pool, ssd, att, x, gtx, scx, shx, g, wo, rw, rb, mod_map):
    t = x.shape[0]
    tm = TOKEN_TILE
    tok = lambda i: (i, 0)
    const = lambda i: (0, 0)
    return pl.pallas_call(
        _mixout_kernel,
        grid=(t // tm,),
        in_specs=[pl.BlockSpec((tm, POOL_WIDTH), tok),
                  pl.BlockSpec((tm, SSD_WIDTH), tok),
                  pl.BlockSpec((tm, ATT_WIDTH), tok),
                  pl.BlockSpec((tm, D_MODEL), tok),
                  pl.BlockSpec((tm, D_MODEL), mod_map),
                  pl.BlockSpec((tm, D_MODEL), mod_map),
                  pl.BlockSpec((tm, D_MODEL), mod_map),
                  pl.BlockSpec((1, D_MODEL), const),
                  pl.BlockSpec((D_MODEL, D_MODEL), const),
                  pl.BlockSpec((D_MODEL, LANES), const),
                  pl.BlockSpec((1, LANES), const)],
        out_specs=[pl.BlockSpec((tm, D_MODEL), tok),
                   pl.BlockSpec((tm, D_MODEL), tok),
                   pl.BlockSpec((tm, LANES), tok)],
        out_shape=[jax.ShapeDtypeStruct((t, D_MODEL), F32),
                   jax.ShapeDtypeStruct((t, D_MODEL), F32),
                   jax.ShapeDtypeStruct((t, LANES), F32)],
        compiler_params=_params(("arbitrary",)),
    )(pool, ssd, att, x, gtx, scx, shx, g, wo, rw, rb)


def _final_norm_kernel(x_ref, g_ref, o_ref):
    o_ref[...] = _rms(x_ref[...], g_ref[...])


def _final_norm(x, g):
    t = x.shape[0]
    tm = TOKEN_TILE
    return pl.pallas_call(
        _final_norm_kernel,
        grid=(t // tm,),
        in_specs=[pl.BlockSpec((tm, D_MODEL), lambda i: (i, 0)),
                  pl.BlockSpec((1, D_MODEL), lambda i: (0, 0))],
        out_specs=pl.BlockSpec((tm, D_MODEL), lambda i: (i, 0)),
        out_shape=jax.ShapeDtypeStruct((t, D_MODEL), F32),
        compiler_params=_params(("arbitrary",)),
    )(x, g)


_POOL_PAD = 8
_POOL_CARRY = 16


def _pool_select(sums, cnts, u):
    lane = lax.broadcasted_iota(jnp.int32, u.shape, u.ndim - 1)
    pooled = sums[-1] / cnts[-1]
    for g in range(len(POOL_WINDOWS) - 2, -1, -1):
        pooled = jnp.where(lane < (g + 1) * POOL_GROUP, sums[g] / cnts[g], pooled)
    return pooled - u


def _pool_prompt_kernel(u_ref, w_ref, scale_ref, o_ref, st_ref, ext_ref, lv_a, lv_b, *, tl):
    j = pl.program_id(1)
    base = _POOL_PAD + _POOL_CARRY
    n = base + tl
    zpad = jnp.zeros((_POOL_PAD, POOL_WIDTH), F32)
    ext_ref[0:_POOL_PAD, :] = zpad
    lv_a[0:_POOL_PAD, :] = zpad
    lv_b[0:_POOL_PAD, :] = zpad

    @pl.when(j == 0)
    def _():
        ext_ref[_POOL_PAD:base, :] = jnp.zeros((_POOL_CARRY, POOL_WIDTH), F32)

    u = u_ref[...]
    ext_ref[base:n, :] = u
    lo = _POOL_PAD
    lv_a[lo:n, :] = ext_ref[lo:n, :] + ext_ref[lo - 1:n - 1, :]
    s2 = lv_a[base:n, :]
    lv_b[lo:n, :] = lv_a[lo:n, :] + lv_a[lo - 2:n - 2, :]
    s4 = lv_b[base:n, :]
    lv_a[lo:n, :] = lv_b[lo:n, :] + lv_b[lo - 4:n - 4, :]
    s8 = lv_a[base:n, :]
    s16 = lv_a[base:n, :] + lv_a[base - 8:n - 8, :]
    pos1 = (lax.broadcasted_iota(jnp.int32, (tl, POOL_WIDTH), 0) + j * tl + 1).astype(F32)
    cnts = [jnp.minimum(pos1, float(w)) for w in POOL_WINDOWS]
    diff = _pool_select([s2, s4, s8, s16], cnts, u)
    o_ref[...] = _dot(diff.astype(BF16), w_ref[...]) * scale_ref[...]
    last = ext_ref[n - _POOL_CARRY:n, :]
    st_ref[0] = last
    ext_ref[_POOL_PAD:base, :] = last


def _pool_prompt(u, wbd, scale, batch, seq):
    tl = 512
    nj = seq // tl
    kern = functools.partial(_pool_prompt_kernel, tl=tl)
    rows = _POOL_PAD + _POOL_CARRY + tl
    return pl.pallas_call(
        kern,
        grid=(batch, nj),
        in_specs=[pl.BlockSpec((tl, POOL_WIDTH), lambda b, j: (b * nj + j, 0)),
                  pl.BlockSpec((POOL_WIDTH, POOL_WIDTH), lambda b, j: (0, 0)),
                  pl.BlockSpec((1, POOL_WIDTH), lambda b, j: (0, 0))],
        out_specs=[pl.BlockSpec((tl, POOL_WIDTH), lambda b, j: (b * nj + j, 0)),
                   pl.BlockSpec((1, _POOL_CARRY, POOL_WIDTH), lambda b, j: (b, 0, 0))],
        out_shape=[jax.ShapeDtypeStruct((batch * seq, POOL_WIDTH), F32),
                   jax.ShapeDtypeStruct((batch, _POOL_CARRY, POOL_WIDTH), F32)],
        scratch_shapes=[pltpu.VMEM((rows, POOL_WIDTH), F32)] * 3,
        compiler_params=_params(("arbitrary", "arbitrary")),
    )(u, wbd, scale)


def _pool_sample_kernel(past_ref, u_ref, w_ref, scale_ref, o_ref, *, n_past, steps, start):
    ext = [past_ref[i] for i in range(n_past)] + [u_ref[t] for t in range(steps)]
    for t in range(steps):
        idx = n_past + t
        sums, cnts = [], []
        acc = ext[idx]
        taken = 1
        for w in POOL_WINDOWS:
            while taken < w:
                if idx - taken >= 0:
                    acc = acc + ext[idx - taken]
                taken += 1
            sums.append(acc)
            cnts.append(float(min(start + t + 1, w)))
        diff = _pool_select(sums, cnts, ext[idx])
        o_ref[t] = _dot(diff.astype(BF16), w_ref[...]) * scale_ref[...]


def _pool_sample(past_t, u_t, wbd, scale, start):
    n_past, bs, _ = past_t.shape
    steps = u_t.shape[0]
    kern = functools.partial(_pool_sample_kernel, n_past=n_past, steps=steps, start=start)
    return pl.pallas_call(
        kern,
        out_shape=jax.ShapeDtypeStruct((steps, bs, POOL_WIDTH), F32),
        compiler_params=pltpu.CompilerParams(vmem_limit_bytes=VMEM_LIMIT),
    )(past_t, u_t, wbd, scale)


def _conv_sample_kernel(past_ref, x_ref, w_ref, b_ref, o_ref, *, steps):
    ext = [past_ref[i] for i in range(CONV_W - 1)] + [x_ref[t] for t in range(steps)]
    for t in range(steps):
        acc = b_ref[...]
        for j in range(CONV_W):
            acc = acc + ext[t + j] * w_ref[j:j + 1, :]
        o_ref[t] = _silu(acc)


def _conv_sample(past_t, x_t, cw, cb):
    steps, bs, _ = x_t.shape
    return pl.pallas_call(
        functools.partial(_conv_sample_kernel, steps=steps),
        out_shape=jax.ShapeDtypeStruct((steps, bs, CONV_DIM), F32),
        compiler_params=pltpu.CompilerParams(vmem_limit_bytes=VMEM_LIMIT),
    )(past_t, x_t, cw, cb)


def _softplus(x):
    return jnp.maximum(x, 0.0) + jnp.log(1.0 + jnp.exp(-jnp.abs(x)))


def _ssd_prepare(xc, dt_raw, dtb, alog, seq_len):
    q = xc.shape[0]
    dt = _softplus(dt_raw + dtb)
    dta = dt * (-jnp.exp(alog))
    ti = lax.broadcasted_iota(jnp.int32, (q, q), 0)
    si = lax.broadcasted_iota(jnp.int32, (q, q), 1)
    same = (ti // seq_len) == (si // seq_len)
    causal = same & (si <= ti)
    after = same & (si > ti)
    acum = _dot_exact(causal.astype(F32), dta)
    suf = _dot_exact(after.astype(F32), dta)
    return dt.T, acum, acum.T, suf.T, causal


def _ssd_diag_head(h, xc, cb, causal, acum, acum_t, dt_t):
    col = acum[:, h:h + 1]
    row = acum_t[h:h + 1, :]
    decay = jnp.where(causal, jnp.exp(jnp.minimum(col - row, 0.0)), 0.0)
    mat = cb * decay * dt_t[h:h + 1, :]
    xh = xc[:, h * HEAD_DIM:(h + 1) * HEAD_DIM]
    return _dot(mat.astype(BF16), xh.astype(BF16)), xh


def _ssd_gate_norm(y, z, ng):
    return _rms(y * _silu(z), ng)


def _ssd_prompt_kernel(xbc_ref, dt_ref, z_ref, cw_ref, cb_ref, dtb_ref, alog_ref, dskip_ref, ng_ref,
                       y_ref, hout_ref, ext_ref, h_ref, ybuf_ref):
    c = pl.program_id(1)
    q = SSD_CHUNK

    @pl.when(c == 0)
    def _():
        ext_ref[0:SUBLANES, :] = jnp.zeros((SUBLANES, CONV_DIM), F32)
        h_ref[...] = jnp.zeros_like(h_ref)

    ext_ref[SUBLANES:SUBLANES + q, :] = xbc_ref[...]
    acc = cb_ref[...]
    for j in range(CONV_W):
        lo = SUBLANES - (CONV_W - 1) + j
        acc = acc + ext_ref[lo:lo + q, :] * cw_ref[j:j + 1, :]
    xc = _silu(acc)
    ext_ref[0:SUBLANES, :] = ext_ref[q:q + SUBLANES, :]

    dt_t, acum, acum_t, suf_t, causal = _ssd_prepare(xc, dt_ref[...], dtb_ref[...], alog_ref[...], q)
    x_t = xc[:, 0:SSD_WIDTH].T
    for g in range(SSD_GROUPS):
        bg = xc[:, SSD_WIDTH + g * D_STATE:SSD_WIDTH + (g + 1) * D_STATE].astype(BF16)
        cg = xc[:, SSD_WIDTH + (SSD_GROUPS + g) * D_STATE:SSD_WIDTH + (SSD_GROUPS + g + 1) * D_STATE].astype(BF16)
        cb = _dot_nt(cg, bg)
        for r in range(HEADS_PER_GROUP):
            h = g * HEADS_PER_GROUP + r
            yd, xh = _ssd_diag_head(h, xc, cb, causal, acum, acum_t, dt_t)
            hst = h_ref[h]
            yo = _dot_nt(cg, hst.astype(BF16)) * jnp.exp(acum[:, h:h + 1])
            ybuf_ref[:, h * HEAD_DIM:(h + 1) * HEAD_DIM] = yd + yo + dskip_ref[0:1, h:h + 1] * xh
            wrow = jnp.exp(suf_t[h:h + 1, :]) * dt_t[h:h + 1, :]
            st = _dot((x_t[h * HEAD_DIM:(h + 1) * HEAD_DIM, :] * wrow).astype(BF16), bg)
            total = acum_t[h:h + 1, q - 1:q]
            h_ref[h] = jnp.exp(total) * hst + st
    y_ref[...] = _ssd_gate_norm(ybuf_ref[...], z_ref[...], ng_ref[...])
    hout_ref[0] = h_ref[...]


def _ssd_prompt(xbc, dt, z, cw, cb, dtb, alog, dskip, ng, batch, seq):
    q = SSD_CHUNK
    nc = seq // q
    tok = lambda b, c: (b * nc + c, 0)
    const = lambda b, c: (0, 0)
    return pl.pallas_call(
        _ssd_prompt_kernel,
        grid=(batch, nc),
        in_specs=[pl.BlockSpec((q, CONV_DIM), tok),
                  pl.BlockSpec((q, LANES), tok),
                  pl.BlockSpec((q, SSD_WIDTH), tok),
                  pl.BlockSpec((CONV_W, CONV_DIM), const),
                  pl.BlockSpec((1, CONV_DIM), const),
                  pl.BlockSpec((1, LANES), const),
                  pl.BlockSpec((1, LANES), const),
                  pl.BlockSpec((1, LANES), const),
                  pl.BlockSpec((1, SSD_WIDTH), const)],
        out_specs=[pl.BlockSpec((q, SSD_WIDTH), tok),
                   pl.BlockSpec((1, SSD_HEADS, HEAD_DIM, D_STATE), lambda b, c: (b, 0, 0, 0))],
        out_shape=[jax.ShapeDtypeStruct((batch * seq, SSD_WIDTH), F32),
                   jax.ShapeDtypeStruct((batch, SSD_HEADS, HEAD_DIM, D_STATE), F32)],
        scratch_shapes=[pltpu.VMEM((q + 2 * SUBLANES, CONV_DIM), F32),
                        pltpu.VMEM((SSD_HEADS, HEAD_DIM, D_STATE), F32),
                        pltpu.VMEM((q, SSD_WIDTH), F32)],
        compiler_params=_params(("arbitrary", "arbitrary")),
    )(xbc, dt, z, cw, cb, dtb, alog, dskip, ng)


def _ssd_sample_kernel(xc_ref, dt_ref, z_ref, h0_ref, dtb_ref, alog_ref, dskip_ref, ng_ref,
                       y_ref, hout_ref, ybuf_ref, acum_ref, xw_ref, *, steps):
    rows = xc_ref.shape[0]
    xc = xc_ref[...]
    dt_t, acum, acum_t, suf_t, causal = _ssd_prepare(xc, dt_ref[...], dtb_ref[...], alog_ref[...], steps)
    acum_ref[...] = acum
    x_t = xc[:, 0:SSD_WIDTH].T
    for g in range(SSD_GROUPS):
        bg = xc[:, SSD_WIDTH + g * D_STATE:SSD_WIDTH + (g + 1) * D_STATE].astype(BF16)
        cg = xc[:, SSD_WIDTH + (SSD_GROUPS + g) * D_STATE:SSD_WIDTH + (SSD_GROUPS + g + 1) * D_STATE].astype(BF16)
        cb = _dot_nt(cg, bg)
        for r in range(HEADS_PER_GROUP):
            h = g * HEADS_PER_GROUP + r
            yd, xh = _ssd_diag_head(h, xc, cb, causal, acum, acum_t, dt_t)
            ybuf_ref[:, h * HEAD_DIM:(h + 1) * HEAD_DIM] = yd + dskip_ref[0:1, h:h + 1] * xh
            wrow = jnp.exp(suf_t[h:h + 1, :]) * dt_t[h:h + 1, :]
            xw_ref[h * HEAD_DIM:(h + 1) * HEAD_DIM, :] = x_t[h * HEAD_DIM:(h + 1) * HEAD_DIM, :] * wrow

    lane = lax.broadcasted_iota(jnp.int32, (1, rows), 1)

    def per_seq(b, carry):
        r0 = pl.multiple_of(b * steps, steps)
        in_seq = (lane // steps) == b
        ac = acum_ref[pl.ds(r0, steps), :]
        for h in range(SSD_HEADS):
            g = h // HEADS_PER_GROUP
            c_lo = SSD_WIDTH + (SSD_GROUPS + g) * D_STATE
            b_lo = SSD_WIDTH + g * D_STATE
            hst = h0_ref[b, h]
            cg = xc_ref[pl.ds(r0, steps), c_lo:c_lo + D_STATE].astype(BF16)
            yo = _dot_nt(cg, hst.astype(BF16)) * jnp.exp(ac[:, h:h + 1])
            sl = slice(h * HEAD_DIM, (h + 1) * HEAD_DIM)
            ybuf_ref[pl.ds(r0, steps), sl] = ybuf_ref[pl.ds(r0, steps), sl] + yo
            xw = jnp.where(in_seq, xw_ref[sl, :], 0.0).astype(BF16)
            st = _dot(xw, xc_ref[:, b_lo:b_lo + D_STATE].astype(BF16))
            total = ac[steps - 1:steps, h:h + 1]
            hout_ref[b, h] = jnp.exp(total) * hst + st
        return carry

    lax.fori_loop(0, rows // steps, per_seq, 0)
    y_ref[...] = _ssd_gate_norm(ybuf_ref[...], z_ref[...], ng_ref[...])


def _ssd_sample(xc, dt, z, h0, dtb, alog, dskip, ng, row0, steps):
    nseq = h0.shape[0]
    nb = SEQS_PER_SSD_BLOCK
    rows = nb * steps
    blk0 = row0 // rows
    const = lambda i: (0, 0)
    return pl.pallas_call(
        functools.partial(_ssd_sample_kernel, steps=steps),
        grid=(nseq // nb,),
        in_specs=[pl.BlockSpec((rows, CONV_DIM), lambda i: (i, 0)),
                  pl.BlockSpec((rows, LANES), lambda i: (blk0 + i, 0)),
                  pl.BlockSpec((rows, SSD_WIDTH), lambda i: (blk0 + i, 0)),
                  pl.BlockSpec((nb, SSD_HEADS, HEAD_DIM, D_STATE), lambda i: (i, 0, 0, 0)),
                  pl.BlockSpec((1, LANES), const),
                  pl.BlockSpec((1, LANES), const),
                  pl.BlockSpec((1, LANES), const),
                  pl.BlockSpec((1, SSD_WIDTH), const)],
        out_specs=[pl.BlockSpec((rows, SSD_WIDTH), lambda i: (i, 0)),
                   pl.BlockSpec((nb, SSD_HEADS, HEAD_DIM, D_STATE), lambda i: (i, 0, 0, 0))],
        out_shape=[jax.ShapeDtypeStruct((nseq * steps, SSD_WIDTH), F32),
                   jax.ShapeDtypeStruct((nseq, SSD_HEADS, HEAD_DIM, D_STATE), F32)],
        scratch_shapes=[pltpu.VMEM((rows, SSD_WIDTH), F32),
                        pltpu.VMEM((rows, LANES), F32),
                        pltpu.VMEM((SSD_WIDTH, rows), F32)],
        compiler_params=_params(("arbitrary",)),
    )(xc, dt, z, h0, dtb, alog, dskip, ng)


def _attn_prompt_kernel(q_ref, k_ref, v_ref, o_ref, kprev, vprev, acc, ms, ls):
    i = pl.program_id(1)
    blk = ATT_BLOCK
    tile = ATT_TILE
    scale = HEAD_DIM ** -0.5

    @pl.when(i == 0)
    def _():
        kprev[...] = jnp.zeros_like(kprev)
        vprev[...] = jnp.zeros_like(vprev)

    qi = lax.broadcasted_iota(jnp.int32, (blk, blk), 0)
    kj = lax.broadcasted_iota(jnp.int32, (blk, blk), 1)
    mask_c = kj <= qi
    mask_p_all = kj >= qi
    lane = lax.broadcasted_iota(jnp.int32, (blk, LANES), 1)
    second_head = lane >= HEAD_DIM

    def attend(c, rows, kp, vp, prev_ok):
        qc = q_ref[c, rows, :] * scale
        kc = k_ref[c, rows, :].astype(BF16)
        vc = v_ref[c, rows, :].astype(BF16)
        kp = kp.astype(BF16)
        vp = vp.astype(BF16)
        mask_p = mask_p_all & prev_ok
        out = None
        for hh in range(2):
            own = second_head if hh else ~second_head
            qm = jnp.where(own, qc, 0.0).astype(BF16)
            sc = jnp.where(mask_c, _dot_nt(qm, kc), NEG)
            sp = jnp.where(mask_p, _dot_nt(qm, kp), NEG)
            m = jnp.maximum(jnp.max(sc, axis=-1, keepdims=True), jnp.max(sp, axis=-1, keepdims=True))
            pc = jnp.exp(sc - m)
            pp = jnp.exp(sp - m)
            l = jnp.sum(pc, axis=-1, keepdims=True) + jnp.sum(pp, axis=-1, keepdims=True)
            o = _dot(pc.astype(BF16), vc) + _dot(pp.astype(BF16), vp)
            if out is None:
                out = (o, jnp.broadcast_to(m, (blk, LANES)), jnp.broadcast_to(l, (blk, LANES)))
            else:
                out = (jnp.where(own, o, out[0]), jnp.where(own, m, out[1]), jnp.where(own, l, out[2]))
        return out

    def run_pattern(d, first):
        span = d * blk
        nfold = tile // span

        def body(idx, carry):
            jb = idx // d
            r = idx % d
            start = jb * span + r
            if d == 1:
                start = pl.multiple_of(start, blk)
            rows = pl.ds(start, blk, stride=d) if d > 1 else pl.ds(start, blk)
            last = tile - span
            rows_prev_tile = pl.ds(last + r, blk, stride=d) if d > 1 else pl.ds(last, blk)
            prev_ok = (jb > 0) | (i > 0)
            for c in range(ATT_CHUNKS):
                kp = kprev[c, rows_prev_tile, :]
                vp = vprev[c, rows_prev_tile, :]
                if nfold > 1:
                    pstart = jnp.maximum(start - span, r)
                    if d == 1:
                        pstart = pl.multiple_of(pstart, blk)
                    rows_prev = pl.ds(pstart, blk, stride=d) if d > 1 else pl.ds(pstart, blk)
                    kp = jnp.where(jb > 0, k_ref[c, rows_prev, :], kp)
                    vp = jnp.where(jb > 0, v_ref[c, rows_prev, :], vp)
                o, m, l = attend(c, rows, kp, vp, prev_ok)
                if first:
                    acc[c, rows, :] = o
                    ms[c, rows, :] = m
                    ls[c, rows, :] = l
                else:
                    m_old = ms[c, rows, :]
                    m_new = jnp.maximum(m_old, m)
                    w_old = jnp.exp(m_old - m_new)
                    w_new = jnp.exp(m - m_new)
                    acc[c, rows, :] = w_old * acc[c, rows, :] + w_new * o
                    ls[c, rows, :] = w_old * ls[c, rows, :] + w_new * l
                    ms[c, rows, :] = m_new
            return carry

        lax.fori_loop(0, tile // blk, body, 0)

    for n, (_, d) in enumerate(DILATED):
        run_pattern(d, n == 0)
    for c in range(ATT_CHUNKS):
        o_ref[:, c * LANES:(c + 1) * LANES] = acc[c] / ls[c]
    kprev[...] = k_ref[...]
    vprev[...] = v_ref[...]


def _attn_prompt(q3, k3, v3, batch, seq):
    tile = ATT_TILE
    nt = seq // tile
    spec = pl.BlockSpec((ATT_CHUNKS, tile, LANES), lambda b, i: (0, b * nt + i, 0))
    scratch = pltpu.VMEM((ATT_CHUNKS, tile, LANES), F32)
    return pl.pallas_call(
        _attn_prompt_kernel,
        grid=(batch, nt),
        in_specs=[spec, spec, spec],
        out_specs=pl.BlockSpec((tile, ATT_WIDTH), lambda b, i: (b * nt + i, 0)),
        out_shape=jax.ShapeDtypeStruct((batch * seq, ATT_WIDTH), F32),
        scratch_shapes=[scratch] * 5,
        compiler_params=_params(("arbitrary", "arbitrary")),
    )(q3, k3, v3)


def _sample_multiplicity(steps, wbuf):
    qpos = wbuf + np.arange(steps)[:, None]
    kpos = np.arange(wbuf + steps)[None, :]
    delta = qpos - kpos
    cnt = np.zeros(delta.shape, np.float32)
    for w, d in DILATED:
        cnt += ((delta >= 0) & (delta % d == 0) & (delta // d <= w // d)).astype(np.float32)
    return cnt


def _attn_sample(q, k, v, cache_k, cache_v, layer, row0, steps, prev_out):
    depth, nseq, wbuf = cache_k.shape[0], cache_k.shape[1], cache_k.shape[2]
    chan_major = lambda c: jnp.transpose(c, (0, 1, 3, 4, 2)).reshape(depth, nseq, ATT_WIDTH, wbuf)
    ck = chan_major(cache_k)
    cv = chan_major(cache_v)
    t = q.shape[1]
    view = lambda a: a.reshape(ATT_CHUNKS, t // steps, steps, LANES)
    b0 = row0 // steps
    new = pl.BlockSpec((ATT_CHUNKS, 1, steps, LANES), lambda b: (0, b0 + b, 0, 0))
    window = pl.BlockSpec((1, 1, ATT_WIDTH, wbuf), lambda b: (layer, b, 0, 0))
    cnt = np.tile(_sample_multiplicity(steps, wbuf), (ATT_HEADS, 1))
    cntc = jnp.asarray(cnt[:, :wbuf])
    cntn = jnp.asarray(np.pad(cnt[:, wbuf:], ((0, 0), (0, LANES - steps))))
    rows = ATT_HEADS * steps
    in_specs = [new, new, new, window, window,
                pl.BlockSpec((rows, wbuf), lambda b: (0, 0)),
                pl.BlockSpec((rows, LANES), lambda b: (0, 0))]
    args = [view(q), view(k), view(v), ck, cv, cntc, cntn]
    aliases = {}
    if prev_out is not None:
        in_specs += [pl.BlockSpec(memory_space=pl.ANY)] * 2
        args += list(prev_out)
        aliases = {len(args) - 2: 1, len(args) - 1: 2}
    stacked = jax.ShapeDtypeStruct((depth, nseq, ATT_WIDTH, wbuf), F32)
    return pl.pallas_call(
        functools.partial(_attn_sample_kernel, steps=steps, wbuf=wbuf),
        grid=(nseq,),
        in_specs=in_specs,
        out_specs=[pl.BlockSpec((1, steps, ATT_WIDTH), lambda b: (b, 0, 0)), window, window],
        out_shape=[jax.ShapeDtypeStruct((nseq, steps, ATT_WIDTH), F32), stacked, stacked],
        input_output_aliases=aliases,
        compiler_params=_params(("arbitrary",)),
    )(*args)


def _attn_sample_kernel(*refs, steps, wbuf):
    q_ref, kn_ref, vn_ref, kc_ref, vc_ref, cntc_ref, cntn_ref = refs[:7]
    att_ref, ko_ref, vo_ref = refs[-3:]
    rows = ATT_HEADS * steps
    full = lambda ref: jnp.concatenate([ref[c, 0] for c in range(ATT_CHUNKS)], axis=1)
    q = full(q_ref) * (HEAD_DIM ** -0.5)
    q6 = jnp.concatenate([q] * ATT_HEADS, axis=0)
    row = lax.broadcasted_iota(jnp.int32, (rows, ATT_WIDTH), 0)
    lane = lax.broadcasted_iota(jnp.int32, (rows, ATT_WIDTH), 1)
    own = (lane // HEAD_DIM) == (row // steps)
    qbd = jnp.where(own, q6, 0.0).astype(BF16)
    kc = kc_ref[0, 0]
    vc = vc_ref[0, 0]
    zpad = jnp.zeros((LANES - steps, ATT_WIDTH), F32)
    knp = jnp.concatenate([full(kn_ref), zpad], axis=0)
    vnp = jnp.concatenate([full(vn_ref), zpad], axis=0)
    kn_t = knp.T
    vn_t = vnp.T
    cntc = cntc_ref[...]
    cntn = cntn_ref[...]
    s_c = jnp.where(cntc > 0.0, _dot(qbd, kc.astype(BF16)), NEG)
    s_n = jnp.where(cntn > 0.0, _dot(qbd, kn_t.astype(BF16)), NEG)
    m = jnp.maximum(jnp.max(s_c, axis=-1, keepdims=True), jnp.max(s_n, axis=-1, keepdims=True))
    p_c = cntc * jnp.exp(s_c - m)
    p_n = cntn * jnp.exp(s_n - m)
    l = jnp.sum(p_c, axis=-1, keepdims=True) + jnp.sum(p_n, axis=-1, keepdims=True)
    o = _dot_nt(p_c.astype(BF16), vc.astype(BF16)) + _dot(p_n.astype(BF16), vnp.astype(BF16))
    o = jnp.where(own, o / l, 0.0)
    att_ref[0] = jnp.sum(o.reshape(ATT_HEADS, steps, ATT_WIDTH), axis=0)

    tail = lax.broadcasted_iota(jnp.int32, (ATT_WIDTH, LANES), 1) >= LANES - steps
    for old, new_t, out_ref in ((kc, kn_t, ko_ref), (vc, vn_t, vo_ref)):
        moved = pltpu.roll(old, wbuf - steps, axis=1)
        out_ref[0, 0, :, 0:wbuf - LANES] = moved[:, 0:wbuf - LANES]
        out_ref[0, 0, :, wbuf - LANES:wbuf] = jnp.where(tail, pltpu.roll(new_t, LANES - steps, axis=1),
                                                        moved[:, wbuf - LANES:wbuf])


def _router_kernel(lg_ref, sel_ref, gate_ref, cnt_ref, run_ref):
    i = pl.program_id(0)

    @pl.when(i == 0)
    def _():
        run_ref[...] = jnp.zeros_like(run_ref)

    tm = lg_ref.shape[0]
    work = lg_ref[...]
    lane = lax.broadcasted_iota(jnp.int32, (tm, LANES), 1)
    vals, hots = [], []
    for _ in range(TOP_K):
        m = jnp.max(work, axis=-1, keepdims=True)
        idx = jnp.min(jnp.where(work == m, lane, LANES), axis=-1, keepdims=True)
        hot = lane == idx
        vals.append(m)
        hots.append(hot)
        work = jnp.where(hot, -jnp.inf, work)
    exps = [jnp.exp(v - vals[0]) for v in vals]
    den = exps[0]
    for e in exps[1:]:
        den = den + e
    chosen = hots[0]
    for hot in hots[1:]:
        chosen = chosen | hot
    onehot = chosen.astype(F32)
    ri = lax.broadcasted_iota(jnp.int32, (tm, tm), 0)
    ci = lax.broadcasted_iota(jnp.int32, (tm, tm), 1)
    before = (ci < ri).astype(BF16)
    rank = _dot(before, onehot.astype(BF16)) + run_ref[...]
    sel = jnp.zeros((tm, LANES), jnp.int32)
    gates = jnp.zeros((tm, LANES), F32)
    for k in range(TOP_K):
        e_k = jnp.min(jnp.where(hots[k], lane, LANES), axis=-1, keepdims=True)
        r_k = jnp.sum(jnp.where(hots[k], rank, 0.0), axis=-1, keepdims=True).astype(jnp.int32)
        sel = jnp.where(lane == k, e_k, sel)
        sel = jnp.where(lane == TOP_K + k, r_k, sel)
        gates = jnp.where(lane == k, exps[k] / den, gates)
    sel_ref[...] = sel
    gate_ref[...] = gates
    run_ref[...] = run_ref[...] + jnp.sum(onehot, axis=0, keepdims=True)
    cnt_ref[...] = run_ref[...]


def _router(logits):
    t = logits.shape[0]
    tm = TOKEN_TILE
    tok = lambda i: (i, 0)
    return pl.pallas_call(
        _router_kernel,
        grid=(t // tm,),
        in_specs=[pl.BlockSpec((tm, LANES), tok)],
        out_specs=[pl.BlockSpec((tm, LANES), tok), pl.BlockSpec((tm, LANES), tok),
                   pl.BlockSpec((1, LANES), lambda i: (0, 0))],
        out_shape=[jax.ShapeDtypeStruct((t, LANES), jnp.int32), jax.ShapeDtypeStruct((t, LANES), F32),
                   jax.ShapeDtypeStruct((1, LANES), F32)],
        scratch_shapes=[pltpu.VMEM((1, LANES), F32)],
        compiler_params=_params(("arbitrary",)),
    )(logits)


def _row_copy(src, dst, sem):
    return pltpu.make_async_copy(src, dst, sem)


ROWS_PER_ISSUE = 8


def _dispatch_kernel(dest_ref, fill_ref, nu_ref, h2_ref, xg_ref, zero_ref, sem, zsem, *, n_blocks):
    i = pl.program_id(0)
    tm = h2_ref.shape[0]
    tb = EXPERT_BLOCK

    @pl.when(i == 0)
    def _():
        zero_ref[...] = jnp.zeros_like(zero_ref)

        def fill_copy(start):
            return pltpu.make_async_copy(zero_ref, xg_ref.at[pl.ds(pl.multiple_of(start, tb), tb)], zsem)

        def each_fill(fn):
            def per_expert(e, carry):
                @pl.when(fill_ref[e] >= 0)
                def _():
                    fn(fill_copy(fill_ref[e]))
                return carry

            def per_tail(b, carry):
                fn(fill_copy(b * tb))
                return carry

            lax.fori_loop(0, N_EXPERTS, per_expert, 0)
            lax.fori_loop(nu_ref[0], n_blocks, per_tail, 0)

        each_fill(lambda cp: cp.start())
        each_fill(lambda cp: cp.wait())

    def issue(g, carry):
        for u in range(ROWS_PER_ISSUE):
            r = g * ROWS_PER_ISSUE + u
            for k in range(TOP_K):
                d = dest_ref[(i * tm + r) * TOP_K + k]
                _row_copy(h2_ref.at[pl.ds(r, 1)], xg_ref.at[pl.ds(d, 1)], sem).start()
        return carry

    lax.fori_loop(0, tm // ROWS_PER_ISSUE, issue, 0)

    def drain(g, carry):
        for _ in range(ROWS_PER_ISSUE * TOP_K):
            _row_copy(h2_ref.at[pl.ds(0, 1)], xg_ref.at[pl.ds(0, 1)], sem).wait()
        return carry

    lax.fori_loop(0, tm // ROWS_PER_ISSUE, drain, 0)


def _dispatch(dest_flat, fill_start, n_used, h2, n_blocks):
    t = h2.shape[0]
    tm = TOKEN_TILE
    tb = EXPERT_BLOCK
    return pl.pallas_call(
        functools.partial(_dispatch_kernel, n_blocks=n_blocks),
        grid_spec=pltpu.PrefetchScalarGridSpec(
            num_scalar_prefetch=3,
            grid=(t // tm,),
            in_specs=[pl.BlockSpec((tm, D_MODEL), lambda i, d, f, n: (i, 0))],
            out_specs=pl.BlockSpec(memory_space=pl.ANY),
            scratch_shapes=[pltpu.VMEM((tb, D_MODEL), F32), pltpu.SemaphoreType.DMA(()),
                            pltpu.SemaphoreType.DMA(())]),
        out_shape=jax.ShapeDtypeStruct((n_blocks * tb, D_MODEL), F32),
        compiler_params=_params(("arbitrary",)),
    )(dest_flat, fill_start, n_used, h2)


_CAST_ROWS = 128


def _expert_kernel(be_ref, nu_ref, x_ref, wgu_ref, bgu_ref, wd_ref, bd_ref, y_ref, wgu_bf, wd_bf):
    i = pl.program_id(0)

    @pl.when(i < nu_ref[0])
    def _():
        @pl.when((i == 0) | (be_ref[i] != be_ref[jnp.maximum(i - 1, 0)]))
        def _():
            def cast(c, carry):
                r = pl.multiple_of(c * _CAST_ROWS, _CAST_ROWS)
                wgu_bf[pl.ds(r, _CAST_ROWS), :] = wgu_ref[0, 0, pl.ds(r, _CAST_ROWS), :].astype(BF16)
                wd_bf[pl.ds(r, _CAST_ROWS), :] = wd_ref[0, 0, pl.ds(r, _CAST_ROWS), :].astype(BF16)
                return carry

            lax.fori_loop(0, D_MODEL // _CAST_ROWS, cast, 0)

        gu = _dot(x_ref[...].astype(BF16), wgu_bf[...]) + bgu_ref[0, 0]
        gate = jnp.minimum(gu[:, :D_FF], SWIGLU_LIMIT)
        up = jnp.clip(gu[:, D_FF:], -SWIGLU_LIMIT, SWIGLU_LIMIT)
        act = (up + 1.0) * gate * _sigmoid(SWIGLU_ALPHA * gate)
        y_ref[...] = _dot(act.astype(BF16), wd_bf[...]) + bd_ref[0, 0]

    @pl.when(i >= nu_ref[0])
    def _():
        y_ref[...] = jnp.zeros_like(y_ref)


def _experts(block_e, n_used, xg, wgu, bgu, wd, bd, layer):
    assert D_FF == D_MODEL
    n_rows = xg.shape[0]
    tb = EXPERT_BLOCK
    blk = lambda i, be, nu: (jnp.minimum(i, nu[0] - 1), 0)
    exp4 = lambda i, be, nu: (layer, be[i], 0, 0)
    return pl.pallas_call(
        _expert_kernel,
        grid_spec=pltpu.PrefetchScalarGridSpec(
            num_scalar_prefetch=2,
            grid=(n_rows // tb,),
            in_specs=[pl.BlockSpec((tb, D_MODEL), blk),
                      pl.BlockSpec((1, 1, D_MODEL, 2 * D_FF), exp4),
                      pl.BlockSpec((1, 1, 1, 2 * D_FF), exp4),
                      pl.BlockSpec((1, 1, D_FF, D_MODEL), exp4),
                      pl.BlockSpec((1, 1, 1, D_MODEL), exp4)],
            out_specs=pl.BlockSpec((tb, D_MODEL), lambda i, be, nu: (i, 0)),
            scratch_shapes=[pltpu.VMEM((D_MODEL, 2 * D_FF), BF16), pltpu.VMEM((D_FF, D_MODEL), BF16)]),
        out_shape=jax.ShapeDtypeStruct((n_rows, D_MODEL), F32),
        compiler_params=_params(("arbitrary",), vmem=EXPERT_VMEM_LIMIT),
    )(block_e, n_used, xg, wgu, bgu, wd, bd)


def _combine_kernel(dest_ref, yg_ref, x1_ref, gate_ref, gt_ref, o_ref, buf_ref, sem):
    i = pl.program_id(0)
    tm = x1_ref.shape[0]

    def issue(g, carry):
        for u in range(ROWS_PER_ISSUE):
            r = g * ROWS_PER_ISSUE + u
            for k in range(TOP_K):
                d = dest_ref[(i * tm + r) * TOP_K + k]
                _row_copy(yg_ref.at[pl.ds(d, 1)], buf_ref.at[k, pl.ds(r, 1)], sem).start()
        return carry

    lax.fori_loop(0, tm // ROWS_PER_ISSUE, issue, 0)

    def drain(g, carry):
        for _ in range(ROWS_PER_ISSUE * TOP_K):
            _row_copy(yg_ref.at[pl.ds(0, 1)], buf_ref.at[0, pl.ds(0, 1)], sem).wait()
        return carry

    lax.fori_loop(0, tm // ROWS_PER_ISSUE, drain, 0)
    gates = gate_ref[...]
    ff = gates[:, 0:1] * buf_ref[0]
    for k in range(1, TOP_K):
        ff = ff + gates[:, k:k + 1] * buf_ref[k]
    o_ref[...] = x1_ref[...] + gt_ref[...] * ff


def _combine(dest_flat, yg, x1, gates, gtx, mod_map):
    t = x1.shape[0]
    tm = TOKEN_TILE
    tok = lambda i, d: (i, 0)
    return pl.pallas_call(
        _combine_kernel,
        grid_spec=pltpu.PrefetchScalarGridSpec(
            num_scalar_prefetch=1,
            grid=(t // tm,),
            in_specs=[pl.BlockSpec(memory_space=pl.ANY),
                      pl.BlockSpec((tm, D_MODEL), tok),
                      pl.BlockSpec((tm, LANES), tok),
                      pl.BlockSpec((tm, D_MODEL), lambda i, d: mod_map(i))],
            out_specs=pl.BlockSpec((tm, D_MODEL), tok),
            scratch_shapes=[pltpu.VMEM((TOP_K, tm, D_MODEL), F32), pltpu.SemaphoreType.DMA(())]),
        out_shape=jax.ShapeDtypeStruct((t, D_MODEL), F32),
        compiler_params=_params(("arbitrary",)),
    )(dest_flat, yg, x1, gates, gtx)


def _route_tables(sel, counts, n_blocks):
    tb = EXPERT_BLOCK
    cnt = counts[0, :N_EXPERTS].astype(jnp.int32)
    padded = (cnt + tb - 1) // tb * tb
    ends = jnp.cumsum(padded)
    starts = ends - padded
    experts = sel[:, :TOP_K]
    ranks = sel[:, TOP_K:2 * TOP_K]
    eid = jnp.arange(N_EXPERTS, dtype=jnp.int32)
    dest = (ranks + jnp.sum(jnp.where(experts[..., None] == eid, starts, 0), axis=-1)).reshape(-1)
    blk_start = jnp.arange(n_blocks, dtype=jnp.int32) * tb
    block_e = jnp.minimum(jnp.sum((ends[None, :] <= blk_start[:, None]).astype(jnp.int32), axis=1), N_EXPERTS - 1)
    n_used = (ends[-1:] // tb).astype(jnp.int32)
    fill_start = jnp.where(cnt % tb != 0, ends - tb, -1)
    return dest.astype(jnp.int32), block_e.astype(jnp.int32), n_used, fill_start.astype(jnp.int32)


def _pad_lanes(v, fill=0.0):
    v = v.reshape(1, -1).astype(F32)
    return jnp.pad(v, ((0, 0), (0, LANES - v.shape[1])), constant_values=fill)


def _pack_w_in(w):
    cuts = np.cumsum([0, POOL_WIDTH, SSD_WIDTH, CONV_DIM, SSD_HEADS, ATT_WIDTH, ATT_WIDTH, ATT_WIDTH])
    pieces = []
    for i, (_, width) in enumerate(_IN_COLS):
        piece = w[:, cuts[i]:cuts[i + 1]]
        pieces.append(jnp.pad(piece, ((0, 0), (0, width - piece.shape[1]))))
    return jnp.concatenate(pieces, axis=1).astype(BF16)


def _block_diag(pool_w):
    g, c, _ = pool_w.shape
    out = jnp.zeros((g * c, g * c), F32)
    for i in range(g):
        out = out.at[i * c:(i + 1) * c, i * c:(i + 1) * c].set(pool_w[i])
    return out.astype(BF16)


def kernel(x_prompt, x_sample, state_pool, state_conv, state_ssm, cache_k, cache_v, c_prompt, c_sample, g_mix, w_ada, b_ada, w_in, pool_w, pool_scale, conv_w, conv_b, dt_bias, a_log, d_skip, ssd_norm_g, w_out, g_ffn, router_w, router_b, w_gate_up, b_gate_up, w_down, b_down, g_final):
    batch, seq, _ = x_prompt.shape
    nseq, steps, _ = x_sample.shape
    depth = w_in.shape[0]
    wbuf = cache_k.shape[2]
    tp = batch * seq
    ts = nseq * steps
    t = tp + ts
    tm = TOKEN_TILE
    assert steps == SUBLANES and seq % (tm * 2) == 0 and ts % tm == 0 and nseq % SEQS_PER_SSD_BLOCK == 0
    assert seq % (DILATED[-1][1] * ATT_BLOCK) == 0 and t % DILATED[-1][1] == 0

    c_all = jnp.concatenate([c_prompt, c_sample], axis=0)
    pad_rows = (-c_all.shape[0]) % SUBLANES
    c_all = jnp.pad(c_all, ((0, pad_rows), (0, 0)))
    mod = _ada(c_all, w_ada, b_ada)

    tiles_per_seq = seq // tm
    n_prompt_tiles = tp // tm

    def mod_map(i):
        return (jnp.where(i < n_prompt_tiles, i // tiles_per_seq, batch + i - n_prompt_tiles), 0)

    def expand(m):
        mp = jnp.repeat(m[:batch], tm, axis=0)
        ms = jnp.repeat(m[batch:batch + nseq], steps, axis=0)
        return jnp.concatenate([mp, ms], axis=0)

    x = jnp.concatenate([x_prompt.reshape(tp, D_MODEL), x_sample.reshape(ts, D_MODEL)], axis=0)

    n_blocks = -(-(t * TOP_K) // EXPERT_BLOCK) + N_EXPERTS
    n_rows = n_blocks * EXPERT_BLOCK

    pool_p, pool_s, conv_p, conv_s, ssm_p, ssm_s, k_p, v_p = [], [], [], [], [], [], [], []
    kv_s = None
    for l in range(depth):
        sh1, sc1, gt1, sh2, sc2, gt2 = [expand(mod[l, :, i * D_MODEL:(i + 1) * D_MODEL]) for i in range(6)]
        u, z, xbc, dt, q, k, v = _inproj(x, sc1, sh1, g_mix[l].reshape(1, -1), _pack_w_in(w_in[l]), mod_map)

        wbd = _block_diag(pool_w[l])
        pscale = pool_scale[l].reshape(1, -1)
        cw, cb = conv_w[l], conv_b[l].reshape(1, -1)
        dtb, alog, dskip = _pad_lanes(dt_bias[l]), _pad_lanes(a_log[l]), _pad_lanes(d_skip[l])
        ng = ssd_norm_g[l].reshape(1, -1)

        pool_out_p, pool_last = _pool_prompt(u, wbd, pscale, batch, seq)
        ssd_out_p, ssm_new_p = _ssd_prompt(xbc, dt, z, cw, cb, dtb, alog, dskip, ng, batch, seq)
        att_p = _attn_prompt(q, k, v, batch, seq)

        u_s = u[tp:].reshape(nseq, steps, POOL_WIDTH)
        xbc_s = xbc[tp:].reshape(nseq, steps, CONV_DIM)
        tl = lambda a: jnp.swapaxes(a, 0, 1)
        pool_out_s = tl(_pool_sample(tl(state_pool[l]), tl(u_s), wbd, pscale, wbuf)).reshape(ts, POOL_WIDTH)
        xc_s = tl(_conv_sample(tl(state_conv[l]), tl(xbc_s), cw, cb)).reshape(ts, CONV_DIM)
        ssd_out_s, ssm_new_s = _ssd_sample(xc_s, dt, z, state_ssm[l], dtb, alog, dskip, ng, tp, steps)
        att_s, ks_new, vs_new = _attn_sample(q, k, v, cache_k, cache_v, l, tp, steps, kv_s)
        kv_s = (ks_new, vs_new)

        pool_all = jnp.concatenate([pool_out_p, pool_out_s], axis=0)
        ssd_all = jnp.concatenate([ssd_out_p, ssd_out_s], axis=0)
        att_all = jnp.concatenate([att_p, att_s.reshape(ts, ATT_WIDTH)], axis=0)

        rw = jnp.pad(router_w[l], ((0, 0), (0, LANES - N_EXPERTS)))
        rb = _pad_lanes(router_b[l], fill=NEG)
        x1, h2, logits = _mixout(pool_all, ssd_all, att_all, x, gt1, sc2, sh2, g_ffn[l].reshape(1, -1),
                                 w_out[l].astype(BF16), rw, rb, mod_map)
        sel, gates, counts = _router(logits)
        dest, block_e, n_used, fill_start = _route_tables(sel, counts, n_blocks)
        xg = _dispatch(dest, fill_start, n_used, h2, n_blocks)
        yg = _experts(block_e, n_used, xg, w_gate_up, b_gate_up.reshape(depth, N_EXPERTS, 1, -1),
                      w_down, b_down.reshape(depth, N_EXPERTS, 1, -1), l)
        x = _combine(dest, yg, x1, gates, gt2, mod_map)

        pool_p.append(pool_last[:, _POOL_CARRY - POOL_KEEP:])
        pool_s.append(jnp.concatenate([state_pool[l], u_s], axis=1)[:, -POOL_KEEP:])
        conv_p.append(xbc[:tp].reshape(batch, seq, CONV_DIM)[:, -(CONV_W - 1):])
        conv_s.append(jnp.concatenate([state_conv[l], xbc_s], axis=1)[:, -(CONV_W - 1):])
        ssm_p.append(ssm_new_p)
        ssm_s.append(ssm_new_s)
        keep = min(MAX_WINDOW, seq)
        kept = lambda a: jnp.transpose(a[:, :tp].reshape(ATT_CHUNKS, batch, seq, LANES)[:, :, -keep:],
                                       (1, 2, 0, 3)).reshape(batch, keep, ATT_HEADS, HEAD_DIM)
        k_p.append(kept(k))
        v_p.append(kept(v))

    y = _final_norm(x, g_final.reshape(1, -1))
    y_prompt = y[:tp].reshape(batch, seq, D_MODEL)
    y_sample = y[tp:].reshape(nseq, steps, D_MODEL)
    pos_major = lambda c: jnp.transpose(c.reshape(depth, nseq, ATT_HEADS, HEAD_DIM, wbuf), (0, 1, 4, 2, 3))
    k_s = pos_major(kv_s[0])
    v_s = pos_major(kv_s[1])
    st = lambda xs: jnp.stack(xs, axis=0)
    return (y_prompt, y_sample, st(pool_p), st(pool_s), st(conv_p), st(conv_s), st(ssm_p), st(ssm_s),
            st(k_p), k_s, st(v_p), v_s)
```

```python
import functools

import numpy as np
import jax
import jax.numpy as jnp
from jax import lax
from jax.experimental import pallas as pl
from jax.experimental.pallas import tpu as pltpu

F32 = jnp.float32
BF16 = jnp.bfloat16

D_MODEL = 1024
HEAD_DIM = 64
POOL_WINDOWS = (2, 4, 8, 16)
POOL_WIDTH = 256
POOL_GROUP = 64
POOL_KEEP = 15
SSD_HEADS = 6
SSD_WIDTH = 384
SSD_GROUPS = 2
HEADS_PER_GROUP = SSD_HEADS // SSD_GROUPS
D_STATE = 64
CONV_W = 4
CONV_DIM = 640
SSD_CHUNK = 128
ATT_HEADS = 6
ATT_WIDTH = 384
DILATED = ((128, 1), (512, 4), (2048, 16))
MAX_WINDOW = 2048
ATT_BLOCK = 128
N_EXPERTS = 32
TOP_K = 4
D_FF = 1024
SWIGLU_LIMIT = 7.0
SWIGLU_ALPHA = 1.702
EPS = 1e-6

LANES = 128
SUBLANES = 8
TOKEN_TILE = 256
EXPERT_BLOCK = 256
SEQS_PER_SSD_BLOCK = 16
VMEM_LIMIT = 48 * 1024 * 1024
EXPERT_VMEM_LIMIT = 56 * 1024 * 1024
NEG = -1e30

_IN_COLS = (("u", POOL_WIDTH), ("z", SSD_WIDTH), ("xbc", CONV_DIM), ("dt", LANES),
            ("q", ATT_WIDTH), ("k", ATT_WIDTH), ("v", ATT_WIDTH))
IN_PACKED = sum(w for _, w in _IN_COLS)
_ATT_NAMES = ("q", "k", "v")
ATT_CHUNKS = ATT_WIDTH // LANES
ATT_TILE = DILATED[-1][1] * ATT_BLOCK


def _params(sem, vmem=VMEM_LIMIT):
    return pltpu.CompilerParams(dimension_semantics=sem, vmem_limit_bytes=vmem)


def _sigmoid(x):
    return 1.0 / (1.0 + jnp.exp(-x))


def _silu(x):
    return x * _sigmoid(x)


def _dot(a, b):
    return jnp.dot(a, b, preferred_element_type=F32)


def _dot_nt(a, b):
    return lax.dot_general(a, b, (((1,), (1,)), ((), ())), preferred_element_type=F32)


def _dot_exact(a, b):
    return jnp.dot(a, b, preferred_element_type=F32, precision=lax.Precision.HIGHEST)


def _rms(x, g):
    ms = jnp.mean(x * x, axis=-1, keepdims=True)
    return x * lax.rsqrt(ms + EPS) * g


def _ada_kernel(c_ref, w_ref, b_ref, o_ref):
    sc = _silu(c_ref[...]).astype(BF16)
    o_ref[0] = _dot(sc, w_ref[0].astype(BF16)) + b_ref[0]


def _ada(c_all, w_ada, b_ada):
    depth, _, n = w_ada.shape
    rows = c_all.shape[0]
    tn = 512
    return pl.pallas_call(
        _ada_kernel,
        grid=(depth, n // tn),
        in_specs=[pl.BlockSpec((rows, D_MODEL), lambda l, j: (0, 0)),
                  pl.BlockSpec((1, D_MODEL, tn), lambda l, j: (l, 0, j)),
                  pl.BlockSpec((1, 1, tn), lambda l, j: (l, 0, j))],
        out_specs=pl.BlockSpec((1, rows, tn), lambda l, j: (l, 0, j)),
        out_shape=jax.ShapeDtypeStruct((depth, rows, n), F32),
        compiler_params=_params(("arbitrary", "arbitrary")),
    )(c_all, w_ada, b_ada.reshape(depth, 1, n))


def _inproj_kernel(x_ref, sc_ref, sh_ref, g_ref, w_ref, u_ref, z_ref, xbc_ref, dt_ref, q_ref, k_ref, v_ref):
    h = _rms(x_ref[...], g_ref[...]) * (1.0 + sc_ref[...]) + sh_ref[...]
    r = _dot(h.astype(BF16), w_ref[...])
    off = 0
    for ref, (name, w) in zip((u_ref, z_ref, xbc_ref, dt_ref, q_ref, k_ref, v_ref), _IN_COLS):
        if name in _ATT_NAMES:
            for c in range(ATT_CHUNKS):
                ref[c] = r[:, off + c * LANES:off + (c + 1) * LANES]
        else:
            ref[...] = r[:, off:off + w]
        off += w


def _inproj(x, scx, shx, g, w_packed, mod_map):
    t = x.shape[0]
    tm = TOKEN_TILE
    tok = lambda i: (i, 0)
    const = lambda i: (0, 0)
    out_specs, out_shape = [], []
    for name, w in _IN_COLS:
        if name in _ATT_NAMES:
            out_specs.append(pl.BlockSpec((ATT_CHUNKS, tm, LANES), lambda i: (0, i, 0)))
            out_shape.append(jax.ShapeDtypeStruct((ATT_CHUNKS, t, LANES), F32))
        else:
            out_specs.append(pl.BlockSpec((tm, w), tok))
            out_shape.append(jax.ShapeDtypeStruct((t, w), F32))
    return pl.pallas_call(
        _inproj_kernel,
        grid=(t // tm,),
        in_specs=[pl.BlockSpec((tm, D_MODEL), tok),
                  pl.BlockSpec((tm, D_MODEL), mod_map),
                  pl.BlockSpec((tm, D_MODEL), mod_map),
                  pl.BlockSpec((1, D_MODEL), const),
                  pl.BlockSpec((D_MODEL, IN_PACKED), const)],
        out_specs=out_specs,
        out_shape=out_shape,
        compiler_params=_params(("arbitrary",)),
    )(x, scx, shx, g, w_packed)


def _split_bf16(a):
    hi = a.astype(BF16)
    lo = (a - hi.astype(F32)).astype(BF16)
    return hi, lo


def _mixout_kernel(pool_p, pool_s, ssd_p, ssd_s, att_p, att_s, x_ref, gt_ref, sc_ref, sh_ref, g_ref, wo_ref,
                   rwh_ref, rwl_ref, rb_ref, x1_ref, h2_ref, lg_ref, *, n_prompt_tiles):
    is_prompt = pl.program_id(0) < n_prompt_tiles
    pick = lambda p_ref, s_ref: jnp.where(is_prompt, p_ref[...], s_ref[...]).astype(BF16)
    mix = _dot(pick(pool_p, pool_s), wo_ref[0:POOL_WIDTH, :])
    mix = mix + _dot(pick(ssd_p, ssd_s), wo_ref[POOL_WIDTH:POOL_WIDTH + SSD_WIDTH, :])
    mix = mix + _dot(pick(att_p, att_s), wo_ref[POOL_WIDTH + SSD_WIDTH:, :])
    x1 = x_ref[...] + gt_ref[...] * mix
    x1_ref[...] = x1
    h2 = _rms(x1, g_ref[...]) * (1.0 + sc_ref[...]) + sh_ref[...]
    h2_ref[...] = h2
    h_hi, h_lo = _split_bf16(h2)
    lg = _dot(h_hi, rwh_ref[...]) + (_dot(h_hi, rwl_ref[...]) + _dot(h_lo, rwh_ref[...]))
    lg_ref[...] = lg + rb_ref[...]


def _mixout(pool, ssd, att, x, gtx, scx, shx, g, wo, rw, rb, mod_map, n_prompt_tiles):
    t = x.shape[0]
    tm = TOKEN_TILE
    tok = lambda i: (i, 0)
    const = lambda i: (0, 0)
    ptok = lambda i: (jnp.minimum(i, n_prompt_tiles - 1), 0)
    stok = lambda i: (jnp.maximum(i - n_prompt_tiles, 0), 0)
    rw_hi, rw_lo = _split_bf16(rw)
    pair = lambda w: [pl.BlockSpec((tm, w), ptok), pl.BlockSpec((tm, w), stok)]
    return pl.pallas_call(
        functools.partial(_mixout_kernel, n_prompt_tiles=n_prompt_tiles),
        grid=(t // tm,),
        in_specs=pair(POOL_WIDTH) + pair(SSD_WIDTH) + pair(ATT_WIDTH) + [
                  pl.BlockSpec((tm, D_MODEL), tok),
                  pl.BlockSpec((tm, D_MODEL), mod_map),
                  pl.BlockSpec((tm, D_MODEL), mod_map),
                  pl.BlockSpec((tm, D_MODEL), mod_map),
                  pl.BlockSpec((1, D_MODEL), const),
                  pl.BlockSpec((D_MODEL, D_MODEL), const),
                  pl.BlockSpec((D_MODEL, LANES), const),
                  pl.BlockSpec((D_MODEL, LANES), const),
                  pl.BlockSpec((1, LANES), const)],
        out_specs=[pl.BlockSpec((tm, D_MODEL), tok),
                   pl.BlockSpec((tm, D_MODEL), tok),
                   pl.BlockSpec((tm, LANES), tok)],
        out_shape=[jax.ShapeDtypeStruct((t, D_MODEL), F32),
                   jax.ShapeDtypeStruct((t, D_MODEL), F32),
                   jax.ShapeDtypeStruct((t, LANES), F32)],
        compiler_params=_params(("arbitrary",)),
    )(pool[0], pool[1], ssd[0], ssd[1], att[0], att[1], x, gtx, scx, shx, g, wo, rw_hi, rw_lo, rb)


def _final_norm_kernel(x_ref, g_ref, o_ref):
    o_ref[...] = _rms(x_ref[...], g_ref[...])


def _final_norm(x, g, row0, rows):
    tm = TOKEN_TILE
    blk0 = row0 // tm
    return pl.pallas_call(
        _final_norm_kernel,
        grid=(rows // tm,),
        in_specs=[pl.BlockSpec((tm, D_MODEL), lambda i: (blk0 + i, 0)),
                  pl.BlockSpec((1, D_MODEL), lambda i: (0, 0))],
        out_specs=pl.BlockSpec((tm, D_MODEL), lambda i: (i, 0)),
        out_shape=jax.ShapeDtypeStruct((rows, D_MODEL), F32),
        compiler_params=_params(("arbitrary",)),
    )(x, g)


_POOL_PAD = 8
_POOL_CARRY = 16


def _pool_select(sums, cnts, u):
    lane = lax.broadcasted_iota(jnp.int32, u.shape, u.ndim - 1)
    pooled = sums[-1] / cnts[-1]
    for g in range(len(POOL_WINDOWS) - 2, -1, -1):
        pooled = jnp.where(lane < (g + 1) * POOL_GROUP, sums[g] / cnts[g], pooled)
    return pooled - u


def _pool_prompt_kernel(u_ref, w_ref, scale_ref, o_ref, st_ref, ext_ref, lv_a, lv_b, *, tl):
    j = pl.program_id(1)
    base = _POOL_PAD + _POOL_CARRY
    n = base + tl
    zpad = jnp.zeros((_POOL_PAD, POOL_WIDTH), F32)
    ext_ref[0:_POOL_PAD, :] = zpad
    lv_a[0:_POOL_PAD, :] = zpad
    lv_b[0:_POOL_PAD, :] = zpad

    @pl.when(j == 0)
    def _():
        ext_ref[_POOL_PAD:base, :] = jnp.zeros((_POOL_CARRY, POOL_WIDTH), F32)

    u = u_ref[...]
    ext_ref[base:n, :] = u
    lo = _POOL_PAD
    lv_a[lo:n, :] = ext_ref[lo:n, :] + ext_ref[lo - 1:n - 1, :]
    s2 = lv_a[base:n, :]
    lv_b[lo:n, :] = lv_a[lo:n, :] + lv_a[lo - 2:n - 2, :]
    s4 = lv_b[base:n, :]
    lv_a[lo:n, :] = lv_b[lo:n, :] + lv_b[lo - 4:n - 4, :]
    s8 = lv_a[base:n, :]
    s16 = lv_a[base:n, :] + lv_a[base - 8:n - 8, :]
    pos1 = (lax.broadcasted_iota(jnp.int32, (tl, POOL_WIDTH), 0) + j * tl + 1).astype(F32)
    cnts = [jnp.minimum(pos1, float(w)) for w in POOL_WINDOWS]
    diff = _pool_select([s2, s4, s8, s16], cnts, u)
    o_ref[...] = _dot(diff.astype(BF16), w_ref[...]) * scale_ref[...]
    last = ext_ref[n - _POOL_CARRY:n, :]
    st_ref[0] = last
    ext_ref[_POOL_PAD:base, :] = last


def _pool_prompt(u, wbd, scale, batch, seq):
    tl = 512
    nj = seq // tl
    kern = functools.partial(_pool_prompt_kernel, tl=tl)
    rows = _POOL_PAD + _POOL_CARRY + tl
    return pl.pallas_call(
        kern,
        grid=(batch, nj),
        in_specs=[pl.BlockSpec((tl, POOL_WIDTH), lambda b, j: (b * nj + j, 0)),
                  pl.BlockSpec((POOL_WIDTH, POOL_WIDTH), lambda b, j: (0, 0)),
                  pl.BlockSpec((1, POOL_WIDTH), lambda b, j: (0, 0))],
        out_specs=[pl.BlockSpec((tl, POOL_WIDTH), lambda b, j: (b * nj + j, 0)),
                   pl.BlockSpec((1, _POOL_CARRY, POOL_WIDTH), lambda b, j: (b, 0, 0))],
        out_shape=[jax.ShapeDtypeStruct((batch * seq, POOL_WIDTH), F32),
                   jax.ShapeDtypeStruct((batch, _POOL_CARRY, POOL_WIDTH), F32)],
        scratch_shapes=[pltpu.VMEM((rows, POOL_WIDTH), F32)] * 3,
        compiler_params=_params(("arbitrary", "arbitrary")),
    )(u, wbd, scale)


def _pool_sample_kernel(past_ref, u_ref, w_ref, scale_ref, o_ref, *, n_past, steps, start):
    ext = [past_ref[i] for i in range(n_past)] + [u_ref[t] for t in range(steps)]
    for t in range(steps):
        idx = n_past + t
        sums, cnts = [], []
        acc = ext[idx]
        taken = 1
        for w in POOL_WINDOWS:
            while taken < w:
                if idx - taken >= 0:
                    acc = acc + ext[idx - taken]
                taken += 1
            sums.append(acc)
            cnts.append(float(min(start + t + 1, w)))
        diff = _pool_select(sums, cnts, ext[idx])
        o_ref[t] = _dot(diff.astype(BF16), w_ref[...]) * scale_ref[...]


def _pool_sample(past_t, u_t, wbd, scale, start):
    n_past, bs, _ = past_t.shape
    steps = u_t.shape[0]
    kern = functools.partial(_pool_sample_kernel, n_past=n_past, steps=steps, start=start)
    return pl.pallas_call(
        kern,
        out_shape=jax.ShapeDtypeStruct((steps, bs, POOL_WIDTH), F32),
        compiler_params=pltpu.CompilerParams(vmem_limit_bytes=VMEM_LIMIT),
    )(past_t, u_t, wbd, scale)


def _conv_sample_kernel(past_ref, x_ref, w_ref, b_ref, o_ref, *, steps):
    ext = [past_ref[i] for i in range(CONV_W - 1)] + [x_ref[t] for t in range(steps)]
    for t in range(steps):
        acc = b_ref[...]
        for j in range(CONV_W):
            acc = acc + ext[t + j] * w_ref[j:j + 1, :]
        o_ref[t] = _silu(acc)


def _conv_sample(past_t, x_t, cw, cb):
    steps, bs, _ = x_t.shape
    return pl.pallas_call(
        functools.partial(_conv_sample_kernel, steps=steps),
        out_shape=jax.ShapeDtypeStruct((steps, bs, CONV_DIM), F32),
        compiler_params=pltpu.CompilerParams(vmem_limit_bytes=VMEM_LIMIT),
    )(past_t, x_t, cw, cb)


def _softplus(x):
    return jnp.maximum(x, 0.0) + jnp.log(1.0 + jnp.exp(-jnp.abs(x)))


def _ssd_prepare(xc, dt_raw, dtb, alog, seq_len):
    q = xc.shape[0]
    dt = _softplus(dt_raw + dtb)
    dta = dt * (-jnp.exp(alog))
    ti = lax.broadcasted_iota(jnp.int32, (q, q), 0)
    si = lax.broadcasted_iota(jnp.int32, (q, q), 1)
    same = (ti // seq_len) == (si // seq_len)
    causal = same & (si <= ti)
    after = same & (si > ti)
    acum = _dot_exact(causal.astype(F32), dta)
    suf = _dot_exact(after.astype(F32), dta)
    return dt.T, acum, acum.T, suf.T, causal


def _ssd_diag_head(h, xc, cb, causal, acum, acum_t, dt_t):
    col = acum[:, h:h + 1]
    row = acum_t[h:h + 1, :]
    decay = jnp.where(causal, jnp.exp(jnp.minimum(col - row, 0.0)), 0.0)
    mat = cb * decay * dt_t[h:h + 1, :]
    xh = xc[:, h * HEAD_DIM:(h + 1) * HEAD_DIM]
    return _dot(mat.astype(BF16), xh.astype(BF16)), xh


def _ssd_gate_norm(y, z, ng):
    return _rms(y * _silu(z), ng)


def _ssd_prompt_kernel(xbc_ref, dt_ref, z_ref, cw_ref, cb_ref, dtb_ref, alog_ref, dskip_ref, ng_ref,
                       y_ref, hout_ref, ext_ref, h_ref, ybuf_ref):
    c = pl.program_id(1)
    q = SSD_CHUNK

    @pl.when(c == 0)
    def _():
        ext_ref[0:SUBLANES, :] = jnp.zeros((SUBLANES, CONV_DIM), F32)
        h_ref[...] = jnp.zeros_like(h_ref)

    ext_ref[SUBLANES:SUBLANES + q, :] = xbc_ref[...]
    acc = cb_ref[...]
    for j in range(CONV_W):
        lo = SUBLANES - (CONV_W - 1) + j
        acc = acc + ext_ref[lo:lo + q, :] * cw_ref[j:j + 1, :]
    xc = _silu(acc)
    ext_ref[0:SUBLANES, :] = ext_ref[q:q + SUBLANES, :]

    dt_t, acum, acum_t, suf_t, causal = _ssd_prepare(xc, dt_ref[...], dtb_ref[...], alog_ref[...], q)
    x_t = xc[:, 0:SSD_WIDTH].T
    for g in range(SSD_GROUPS):
        bg = xc[:, SSD_WIDTH + g * D_STATE:SSD_WIDTH + (g + 1) * D_STATE].astype(BF16)
        cg = xc[:, SSD_WIDTH + (SSD_GROUPS + g) * D_STATE:SSD_WIDTH + (SSD_GROUPS + g + 1) * D_STATE].astype(BF16)
        cb = _dot_nt(cg, bg)
        for r in range(HEADS_PER_GROUP):
            h = g * HEADS_PER_GROUP + r
            yd, xh = _ssd_diag_head(h, xc, cb, causal, acum, acum_t, dt_t)
            hst = h_ref[h]
            yo = _dot_nt(cg, hst.astype(BF16)) * jnp.exp(acum[:, h:h + 1])
            ybuf_ref[:, h * HEAD_DIM:(h + 1) * HEAD_DIM] = yd + yo + dskip_ref[0:1, h:h + 1] * xh
            wrow = jnp.exp(suf_t[h:h + 1, :]) * dt_t[h:h + 1, :]
            st = _dot((x_t[h * HEAD_DIM:(h + 1) * HEAD_DIM, :] * wrow).astype(BF16), bg)
            total = acum_t[h:h + 1, q - 1:q]
            h_ref[h] = jnp.exp(total) * hst + st
    y_ref[...] = _ssd_gate_norm(ybuf_ref[...], z_ref[...], ng_ref[...])
    hout_ref[0] = h_ref[...]


def _ssd_prompt(xbc, dt, z, cw, cb, dtb, alog, dskip, ng, batch, seq):
    q = SSD_CHUNK
    nc = seq // q
    tok = lambda b, c: (b * nc + c, 0)
    const = lambda b, c: (0, 0)
    return pl.pallas_call(
        _ssd_prompt_kernel,
        grid=(batch, nc),
        in_specs=[pl.BlockSpec((q, CONV_DIM), tok),
                  pl.BlockSpec((q, LANES), tok),
                  pl.BlockSpec((q, SSD_WIDTH), tok),
                  pl.BlockSpec((CONV_W, CONV_DIM), const),
                  pl.BlockSpec((1, CONV_DIM), const),
                  pl.BlockSpec((1, LANES), const),
                  pl.BlockSpec((1, LANES), const),
                  pl.BlockSpec((1, LANES), const),
                  pl.BlockSpec((1, SSD_WIDTH), const)],
        out_specs=[pl.BlockSpec((q, SSD_WIDTH), tok),
                   pl.BlockSpec((1, SSD_HEADS, HEAD_DIM, D_STATE), lambda b, c: (b, 0, 0, 0))],
        out_shape=[jax.ShapeDtypeStruct((batch * seq, SSD_WIDTH), F32),
                   jax.ShapeDtypeStruct((batch, SSD_HEADS, HEAD_DIM, D_STATE), F32)],
        scratch_shapes=[pltpu.VMEM((q + 2 * SUBLANES, CONV_DIM), F32),
                        pltpu.VMEM((SSD_HEADS, HEAD_DIM, D_STATE), F32),
                        pltpu.VMEM((q, SSD_WIDTH), F32)],
        compiler_params=_params(("arbitrary", "arbitrary")),
    )(xbc, dt, z, cw, cb, dtb, alog, dskip, ng)


def _ssd_sample_kernel(xc_ref, dt_ref, z_ref, h0_ref, dtb_ref, alog_ref, dskip_ref, ng_ref,
                       y_ref, hout_ref, ybuf_ref, acum_ref, xw_ref, *, steps):
    rows = xc_ref.shape[0]
    xc = xc_ref[...]
    dt_t, acum, acum_t, suf_t, causal = _ssd_prepare(xc, dt_ref[...], dtb_ref[...], alog_ref[...], steps)
    acum_ref[...] = acum
    x_t = xc[:, 0:SSD_WIDTH].T
    for g in range(SSD_GROUPS):
        bg = xc[:, SSD_WIDTH + g * D_STATE:SSD_WIDTH + (g + 1) * D_STATE].astype(BF16)
        cg = xc[:, SSD_WIDTH + (SSD_GROUPS + g) * D_STATE:SSD_WIDTH + (SSD_GROUPS + g + 1) * D_STATE].astype(BF16)
        cb = _dot_nt(cg, bg)
        for r in range(HEADS_PER_GROUP):
            h = g * HEADS_PER_GROUP + r
            yd, xh = _ssd_diag_head(h, xc, cb, causal, acum, acum_t, dt_t)
            ybuf_ref[:, h * HEAD_DIM:(h + 1) * HEAD_DIM] = yd + dskip_ref[0:1, h:h + 1] * xh
            wrow = jnp.exp(suf_t[h:h + 1, :]) * dt_t[h:h + 1, :]
            xw_ref[h * HEAD_DIM:(h + 1) * HEAD_DIM, :] = x_t[h * HEAD_DIM:(h + 1) * HEAD_DIM, :] * wrow

    lane = lax.broadcasted_iota(jnp.int32, (1, rows), 1)

    def per_seq(b, carry):
        r0 = pl.multiple_of(b * steps, steps)
        in_seq = (lane // steps) == b
        ac = acum_ref[pl.ds(r0, steps), :]
        for h in range(SSD_HEADS):
            g = h // HEADS_PER_GROUP
            c_lo = SSD_WIDTH + (SSD_GROUPS + g) * D_STATE
            b_lo = SSD_WIDTH + g * D_STATE
            hst = h0_ref[b, h]
            cg = xc_ref[pl.ds(r0, steps), c_lo:c_lo + D_STATE].astype(BF16)
            yo = _dot_nt(cg, hst.astype(BF16)) * jnp.exp(ac[:, h:h + 1])
            sl = slice(h * HEAD_DIM, (h + 1) * HEAD_DIM)
            ybuf_ref[pl.ds(r0, steps), sl] = ybuf_ref[pl.ds(r0, steps), sl] + yo
            xw = jnp.where(in_seq, xw_ref[sl, :], 0.0).astype(BF16)
            st = _dot(xw, xc_ref[:, b_lo:b_lo + D_STATE].astype(BF16))
            total = ac[steps - 1:steps, h:h + 1]
            hout_ref[b, h] = jnp.exp(total) * hst + st
        return carry

    lax.fori_loop(0, rows // steps, per_seq, 0)
    y_ref[...] = _ssd_gate_norm(ybuf_ref[...], z_ref[...], ng_ref[...])


def _ssd_sample(xc, dt, z, h0, dtb, alog, dskip, ng, row0, steps):
    nseq = h0.shape[0]
    nb = SEQS_PER_SSD_BLOCK
    rows = nb * steps
    blk0 = row0 // rows
    const = lambda i: (0, 0)
    return pl.pallas_call(
        functools.partial(_ssd_sample_kernel, steps=steps),
        grid=(nseq // nb,),
        in_specs=[pl.BlockSpec((rows, CONV_DIM), lambda i: (i, 0)),
                  pl.BlockSpec((rows, LANES), lambda i: (blk0 + i, 0)),
                  pl.BlockSpec((rows, SSD_WIDTH), lambda i: (blk0 + i, 0)),
                  pl.BlockSpec((nb, SSD_HEADS, HEAD_DIM, D_STATE), lambda i: (i, 0, 0, 0)),
                  pl.BlockSpec((1, LANES), const),
                  pl.BlockSpec((1, LANES), const),
                  pl.BlockSpec((1, LANES), const),
                  pl.BlockSpec((1, SSD_WIDTH), const)],
        out_specs=[pl.BlockSpec((rows, SSD_WIDTH), lambda i: (i, 0)),
                   pl.BlockSpec((nb, SSD_HEADS, HEAD_DIM, D_STATE), lambda i: (i, 0, 0, 0))],
        out_shape=[jax.ShapeDtypeStruct((nseq * steps, SSD_WIDTH), F32),
                   jax.ShapeDtypeStruct((nseq, SSD_HEADS, HEAD_DIM, D_STATE), F32)],
        scratch_shapes=[pltpu.VMEM((rows, SSD_WIDTH), F32),
                        pltpu.VMEM((rows, LANES), F32),
                        pltpu.VMEM((SSD_WIDTH, rows), F32)],
        compiler_params=_params(("arbitrary",)),
    )(xc, dt, z, h0, dtb, alog, dskip, ng)


def _attn_prompt_kernel(q_ref, k_ref, v_ref, o_ref, kprev, vprev, acc, ms, ls, s_scr, p_scr, m_scr, v_bf):
    i = pl.program_id(1)
    blk = ATT_BLOCK
    tile = ATT_TILE
    scale = HEAD_DIM ** -0.5

    @pl.when(i == 0)
    def _():
        kprev[...] = jnp.zeros_like(kprev)
        vprev[...] = jnp.zeros_like(vprev)

    qi = lax.broadcasted_iota(jnp.int32, (blk, blk), 0)
    kj = lax.broadcasted_iota(jnp.int32, (blk, blk), 1)
    mask_c = kj <= qi
    mask_p_all = kj >= qi
    lane = lax.broadcasted_iota(jnp.int32, (blk, LANES), 1)
    second_head = lane >= HEAD_DIM
    ones = jnp.ones((blk, LANES), BF16)

    nhead = 2 * ATT_CHUNKS

    def run_pattern(d, first):
        span = d * blk
        nfold = tile // span

        def body(idx, carry):
            jb = idx // d
            r = idx % d
            start = jb * span + r
            if d == 1:
                start = pl.multiple_of(start, blk)
            rows = pl.ds(start, blk, stride=d) if d > 1 else pl.ds(start, blk)
            last = tile - span
            rows_prev_tile = pl.ds(last + r, blk, stride=d) if d > 1 else pl.ds(last, blk)
            mask_p = mask_p_all & ((jb > 0) | (i > 0))

            for c in range(ATT_CHUNKS):
                kp = kprev[c, rows_prev_tile, :]
                vp = vprev[c, rows_prev_tile, :]
                if nfold > 1:
                    pstart = jnp.maximum(start - span, r)
                    if d == 1:
                        pstart = pl.multiple_of(pstart, blk)
                    rows_prev = pl.ds(pstart, blk, stride=d) if d > 1 else pl.ds(pstart, blk)
                    kp = jnp.where(jb > 0, k_ref[c, rows_prev, :], kp)
                    vp = jnp.where(jb > 0, v_ref[c, rows_prev, :], vp)
                qc = q_ref[c, rows, :] * scale
                kc = k_ref[c, rows, :].astype(BF16)
                kp = kp.astype(BF16)
                v_bf[2 * c] = v_ref[c, rows, :].astype(BF16)
                v_bf[2 * c + 1] = vp.astype(BF16)
                for hh in range(2):
                    j = 2 * c + hh
                    own = second_head if hh else ~second_head
                    qm = jnp.where(own, qc, 0.0).astype(BF16)
                    s_scr[2 * j] = jnp.where(mask_c, _dot_nt(qm, kc), NEG)
                    s_scr[2 * j + 1] = jnp.where(mask_p, _dot_nt(qm, kp), NEG)
            for j in range(nhead):
                sc = s_scr[2 * j]
                sp = s_scr[2 * j + 1]
                m = jnp.max(jnp.maximum(sc, sp), axis=-1, keepdims=True)
                m_scr[j] = jnp.broadcast_to(m, (blk, LANES))
                p_scr[2 * j] = jnp.exp(sc - m).astype(BF16)
                p_scr[2 * j + 1] = jnp.exp(sp - m).astype(BF16)
            for c in range(ATT_CHUNKS):
                vc = v_bf[2 * c]
                vp = v_bf[2 * c + 1]
                per_head = []
                for hh in range(2):
                    j = 2 * c + hh
                    pc = p_scr[2 * j]
                    pp = p_scr[2 * j + 1]
                    per_head.append((_dot(pc, vc) + _dot(pp, vp), _dot(pc, ones) + _dot(pp, ones), m_scr[j]))
                o, l, m = (jnp.where(second_head, b, a) for a, b in zip(*per_head))
                if first:
                    acc[c, rows, :] = o
                    ms[c, rows, :] = m
                    ls[c, rows, :] = l
                else:
                    m_old = ms[c, rows, :]
                    m_new = jnp.maximum(m_old, m)
                    w_old = jnp.exp(m_old - m_new)
                    w_new = jnp.exp(m - m_new)
                    acc[c, rows, :] = w_old * acc[c, rows, :] + w_new * o
                    ls[c, rows, :] = w_old * ls[c, rows, :] + w_new * l
                    ms[c, rows, :] = m_new
            return carry

        lax.fori_loop(0, tile // blk, body, 0)

    for n, (_, d) in enumerate(DILATED):
        run_pattern(d, n == 0)
    for c in range(ATT_CHUNKS):
        o_ref[:, c * LANES:(c + 1) * LANES] = acc[c] / ls[c]
    kprev[...] = k_ref[...]
    vprev[...] = v_ref[...]


def _attn_prompt(q3, k3, v3, batch, seq):
    tile = ATT_TILE
    nt = seq // tile
    spec = pl.BlockSpec((ATT_CHUNKS, tile, LANES), lambda b, i: (0, b * nt + i, 0))
    scratch = pltpu.VMEM((ATT_CHUNKS, tile, LANES), F32)
    nhead = 2 * ATT_CHUNKS
    return pl.pallas_call(
        _attn_prompt_kernel,
        grid=(batch, nt),
        in_specs=[spec, spec, spec],
        out_specs=pl.BlockSpec((tile, ATT_WIDTH), lambda b, i: (b * nt + i, 0)),
        out_shape=jax.ShapeDtypeStruct((batch * seq, ATT_WIDTH), F32),
        scratch_shapes=[scratch] * 5 + [
            pltpu.VMEM((2 * nhead, ATT_BLOCK, ATT_BLOCK), F32),
            pltpu.VMEM((2 * nhead, ATT_BLOCK, ATT_BLOCK), BF16),
            pltpu.VMEM((nhead, ATT_BLOCK, LANES), F32),
            pltpu.VMEM((2 * ATT_CHUNKS, ATT_BLOCK, LANES), BF16)],
        compiler_params=_params(("arbitrary", "arbitrary")),
    )(q3, k3, v3)


def _sample_multiplicity(steps, wbuf):
    qpos = wbuf + np.arange(steps)[:, None]
    kpos = np.arange(wbuf + steps)[None, :]
    delta = qpos - kpos
    cnt = np.zeros(delta.shape, np.float32)
    for w, d in DILATED:
        cnt += ((delta >= 0) & (delta % d == 0) & (delta // d <= w // d)).astype(np.float32)
    return cnt


def _attn_sample(q, k, v, cache_k, cache_v, layer, row0, steps, prev_out):
    depth, nseq, wbuf = cache_k.shape[0], cache_k.shape[1], cache_k.shape[2]
    chan_major = lambda c: jnp.transpose(c, (0, 1, 3, 4, 2)).reshape(depth, nseq, ATT_WIDTH, wbuf)
    ck = chan_major(cache_k)
    cv = chan_major(cache_v)
    t = q.shape[1]
    view = lambda a: a.reshape(ATT_CHUNKS, t // steps, steps, LANES)
    b0 = row0 // steps
    new = pl.BlockSpec((ATT_CHUNKS, 1, steps, LANES), lambda b: (0, b0 + b, 0, 0))
    window = pl.BlockSpec((1, 1, ATT_WIDTH, wbuf), lambda b: (layer, b, 0, 0))
    cnt = np.tile(_sample_multiplicity(steps, wbuf), (ATT_HEADS, 1))
    cntc = jnp.asarray(cnt[:, :wbuf])
    cntn = jnp.asarray(np.pad(cnt[:, wbuf:], ((0, 0), (0, LANES - steps))))
    rows = ATT_HEADS * steps
    in_specs = [new, new, new, window, window,
                pl.BlockSpec((rows, wbuf), lambda b: (0, 0)),
                pl.BlockSpec((rows, LANES), lambda b: (0, 0))]
    args = [view(q), view(k), view(v), ck, cv, cntc, cntn]
    aliases = {}
    if prev_out is not None:
        in_specs += [pl.BlockSpec(memory_space=pl.ANY)] * 2
        args += list(prev_out)
        aliases = {len(args) - 2: 1, len(args) - 1: 2}
    stacked = jax.ShapeDtypeStruct((depth, nseq, ATT_WIDTH, wbuf), F32)
    return pl.pallas_call(
        functools.partial(_attn_sample_kernel, steps=steps, wbuf=wbuf),
        grid=(nseq,),
        in_specs=in_specs,
        out_specs=[pl.BlockSpec((1, steps, ATT_WIDTH), lambda b: (b, 0, 0)), window, window],
        out_shape=[jax.ShapeDtypeStruct((nseq, steps, ATT_WIDTH), F32), stacked, stacked],
        input_output_aliases=aliases,
        compiler_params=_params(("arbitrary",)),
    )(*args)


def _attn_sample_kernel(*refs, steps, wbuf):
    q_ref, kn_ref, vn_ref, kc_ref, vc_ref, cntc_ref, cntn_ref = refs[:7]
    att_ref, ko_ref, vo_ref = refs[-3:]
    rows = ATT_HEADS * steps
    full = lambda ref: jnp.concatenate([ref[c, 0] for c in range(ATT_CHUNKS)], axis=1)
    q = full(q_ref) * (HEAD_DIM ** -0.5)
    q6 = jnp.concatenate([q] * ATT_HEADS, axis=0)
    row = lax.broadcasted_iota(jnp.int32, (rows, ATT_WIDTH), 0)
    lane = lax.broadcasted_iota(jnp.int32, (rows, ATT_WIDTH), 1)
    own = (lane // HEAD_DIM) == (row // steps)
    qbd = jnp.where(own, q6, 0.0).astype(BF16)
    kc = kc_ref[0, 0]
    vc = vc_ref[0, 0]
    zpad = jnp.zeros((LANES - steps, ATT_WIDTH), F32)
    knp = jnp.concatenate([full(kn_ref), zpad], axis=0)
    vnp = jnp.concatenate([full(vn_ref), zpad], axis=0)
    kn_t = knp.T
    vn_t = vnp.T
    cntc = cntc_ref[...]
    cntn = cntn_ref[...]
    s_c = jnp.where(cntc > 0.0, _dot(qbd, kc.astype(BF16)), NEG)
    s_n = jnp.where(cntn > 0.0, _dot(qbd, kn_t.astype(BF16)), NEG)
    m = jnp.maximum(jnp.max(s_c, axis=-1, keepdims=True), jnp.max(s_n, axis=-1, keepdims=True))
    p_c = cntc * jnp.exp(s_c - m)
    p_n = cntn * jnp.exp(s_n - m)
    l = jnp.sum(p_c, axis=-1, keepdims=True) + jnp.sum(p_n, axis=-1, keepdims=True)
    o = _dot_nt(p_c.astype(BF16), vc.astype(BF16)) + _dot(p_n.astype(BF16), vnp.astype(BF16))
    o = jnp.where(own, o / l, 0.0)
    att_ref[0] = jnp.sum(o.reshape(ATT_HEADS, steps, ATT_WIDTH), axis=0)

    tail = lax.broadcasted_iota(jnp.int32, (ATT_WIDTH, LANES), 1) >= LANES - steps
    for old, new_t, out_ref in ((kc, kn_t, ko_ref), (vc, vn_t, vo_ref)):
        moved = pltpu.roll(old, wbuf - steps, axis=1)
        out_ref[0, 0, :, 0:wbuf - LANES] = moved[:, 0:wbuf - LANES]
        out_ref[0, 0, :, wbuf - LANES:wbuf] = jnp.where(tail, pltpu.roll(new_t, LANES - steps, axis=1),
                                                        moved[:, wbuf - LANES:wbuf])


def _router_kernel(lg_ref, sel_ref, gate_ref, cnt_ref, run_ref):
    i = pl.program_id(0)

    @pl.when(i == 0)
    def _():
        run_ref[...] = jnp.zeros_like(run_ref)

    tm = lg_ref.shape[0]
    work = lg_ref[...]
    lane = lax.broadcasted_iota(jnp.int32, (tm, LANES), 1)
    vals, hots = [], []
    for _ in range(TOP_K):
        m = jnp.max(work, axis=-1, keepdims=True)
        idx = jnp.min(jnp.where(work == m, lane, LANES), axis=-1, keepdims=True)
        hot = lane == idx
        vals.append(m)
        hots.append(hot)
        work = jnp.where(hot, -jnp.inf, work)
    exps = [jnp.exp(v - vals[0]) for v in vals]
    den = exps[0]
    for e in exps[1:]:
        den = den + e
    chosen = hots[0]
    for hot in hots[1:]:
        chosen = chosen | hot
    onehot = chosen.astype(F32)
    ri = lax.broadcasted_iota(jnp.int32, (tm, tm), 0)
    ci = lax.broadcasted_iota(jnp.int32, (tm, tm), 1)
    before = (ci < ri).astype(BF16)
    rank = _dot(before, onehot.astype(BF16)) + run_ref[...]
    sel = jnp.zeros((tm, LANES), jnp.int32)
    gates = jnp.zeros((tm, LANES), F32)
    for k in range(TOP_K):
        e_k = jnp.min(jnp.where(hots[k], lane, LANES), axis=-1, keepdims=True)
        r_k = jnp.sum(jnp.where(hots[k], rank, 0.0), axis=-1, keepdims=True).astype(jnp.int32)
        sel = jnp.where(lane == k, e_k, sel)
        sel = jnp.where(lane == TOP_K + k, r_k, sel)
        gates = jnp.where(lane == k, exps[k] / den, gates)
    sel_ref[...] = sel
    gate_ref[...] = gates
    run_ref[...] = run_ref[...] + jnp.sum(onehot, axis=0, keepdims=True)
    cnt_ref[...] = run_ref[...]


def _router(logits):
    t = logits.shape[0]
    tm = TOKEN_TILE
    tok = lambda i: (i, 0)
    return pl.pallas_call(
        _router_kernel,
        grid=(t // tm,),
        in_specs=[pl.BlockSpec((tm, LANES), tok)],
        out_specs=[pl.BlockSpec((tm, LANES), tok), pl.BlockSpec((tm, LANES), tok),
                   pl.BlockSpec((1, LANES), lambda i: (0, 0))],
        out_shape=[jax.ShapeDtypeStruct((t, LANES), jnp.int32), jax.ShapeDtypeStruct((t, LANES), F32),
                   jax.ShapeDtypeStruct((1, LANES), F32)],
        scratch_shapes=[pltpu.VMEM((1, LANES), F32)],
        compiler_params=_params(("arbitrary",)),
    )(logits)


def _row_copy(src, dst, sem):
    return pltpu.make_async_copy(src, dst, sem)


ROWS_PER_ISSUE = 8


def _dispatch_kernel(dest_ref, fill_ref, nu_ref, h2_ref, xg_ref, zero_ref, sem, zsem, *, n_blocks):
    i = pl.program_id(0)
    tm = h2_ref.shape[0]
    tb = EXPERT_BLOCK

    @pl.when(i == 0)
    def _():
        zero_ref[...] = jnp.zeros_like(zero_ref)

        def fill_copy(start):
            return pltpu.make_async_copy(zero_ref, xg_ref.at[pl.ds(pl.multiple_of(start, tb), tb)], zsem)

        def each_fill(fn):
            def per_expert(e, carry):
                @pl.when(fill_ref[e] >= 0)
                def _():
                    fn(fill_copy(fill_ref[e]))
                return carry

            def per_tail(b, carry):
                fn(fill_copy(b * tb))
                return carry

            lax.fori_loop(0, N_EXPERTS, per_expert, 0)
            lax.fori_loop(nu_ref[0], n_blocks, per_tail, 0)

        each_fill(lambda cp: cp.start())
        each_fill(lambda cp: cp.wait())

    def issue(g, carry):
        for u in range(ROWS_PER_ISSUE):
            r = g * ROWS_PER_ISSUE + u
            for k in range(TOP_K):
                d = dest_ref[(i * tm + r) * TOP_K + k]
                _row_copy(h2_ref.at[pl.ds(r, 1)], xg_ref.at[pl.ds(d, 1)], sem).start()
        return carry

    lax.fori_loop(0, tm // ROWS_PER_ISSUE, issue, 0)

    def drain(g, carry):
        for _ in range(ROWS_PER_ISSUE * TOP_K):
            _row_copy(h2_ref.at[pl.ds(0, 1)], xg_ref.at[pl.ds(0, 1)], sem).wait()
        return carry

    lax.fori_loop(0, tm // ROWS_PER_ISSUE, drain, 0)


def _dispatch(dest_flat, fill_start, n_used, h2, n_blocks):
    t = h2.shape[0]
    tm = TOKEN_TILE
    tb = EXPERT_BLOCK
    return pl.pallas_call(
        functools.partial(_dispatch_kernel, n_blocks=n_blocks),
        grid_spec=pltpu.PrefetchScalarGridSpec(
            num_scalar_prefetch=3,
            grid=(t // tm,),
            in_specs=[pl.BlockSpec((tm, D_MODEL), lambda i, d, f, n: (i, 0))],
            out_specs=pl.BlockSpec(memory_space=pl.ANY),
            scratch_shapes=[pltpu.VMEM((tb, D_MODEL), F32), pltpu.SemaphoreType.DMA(()),
                            pltpu.SemaphoreType.DMA(())]),
        out_shape=jax.ShapeDtypeStruct((n_blocks * tb, D_MODEL), F32),
        compiler_params=_params(("arbitrary",)),
    )(dest_flat, fill_start, n_used, h2)


_CAST_ROWS = 128


def _expert_kernel(be_ref, nu_ref, x_ref, wgu_ref, bgu_ref, wd_ref, bd_ref, y_ref, wgu_bf, wd_bf):
    i = pl.program_id(0)

    @pl.when(i < nu_ref[0])
    def _():
        @pl.when((i == 0) | (be_ref[i] != be_ref[jnp.maximum(i - 1, 0)]))
        def _():
            def cast(c, carry):
                r = pl.multiple_of(c * _CAST_ROWS, _CAST_ROWS)
                wgu_bf[pl.ds(r, _CAST_ROWS), :] = wgu_ref[0, 0, pl.ds(r, _CAST_ROWS), :].astype(BF16)
                wd_bf[pl.ds(r, _CAST_ROWS), :] = wd_ref[0, 0, pl.ds(r, _CAST_ROWS), :].astype(BF16)
                return carry

            lax.fori_loop(0, D_MODEL // _CAST_ROWS, cast, 0)

        gu = _dot(x_ref[...].astype(BF16), wgu_bf[...]) + bgu_ref[0, 0]
        gate = jnp.minimum(gu[:, :D_FF], SWIGLU_LIMIT)
        up = jnp.clip(gu[:, D_FF:], -SWIGLU_LIMIT, SWIGLU_LIMIT)
        act = (up + 1.0) * gate * _sigmoid(SWIGLU_ALPHA * gate)
        y_ref[...] = _dot(act.astype(BF16), wd_bf[...]) + bd_ref[0, 0]

    @pl.when(i >= nu_ref[0])
    def _():
        y_ref[...] = jnp.zeros_like(y_ref)


def _experts(block_e, n_used, xg, wgu, bgu, wd, bd, layer):
    assert D_FF == D_MODEL
    n_rows = xg.shape[0]
    tb = EXPERT_BLOCK
    blk = lambda i, be, nu: (jnp.minimum(i, nu[0] - 1), 0)
    exp4 = lambda i, be, nu: (layer, be[i], 0, 0)
    return pl.pallas_call(
        _expert_kernel,
        grid_spec=pltpu.PrefetchScalarGridSpec(
            num_scalar_prefetch=2,
            grid=(n_rows // tb,),
            in_specs=[pl.BlockSpec((tb, D_MODEL), blk),
                      pl.BlockSpec((1, 1, D_MODEL, 2 * D_FF), exp4),
                      pl.BlockSpec((1, 1, 1, 2 * D_FF), exp4),
                      pl.BlockSpec((1, 1, D_FF, D_MODEL), exp4),
                      pl.BlockSpec((1, 1, 1, D_MODEL), exp4)],
            out_specs=pl.BlockSpec((tb, D_MODEL), lambda i, be, nu: (i, 0)),
            scratch_shapes=[pltpu.VMEM((D_MODEL, 2 * D_FF), BF16), pltpu.VMEM((D_FF, D_MODEL), BF16)]),
        out_shape=jax.ShapeDtypeStruct((n_rows, D_MODEL), F32),
        compiler_params=_params(("arbitrary",), vmem=EXPERT_VMEM_LIMIT),
    )(block_e, n_used, xg, wgu, bgu, wd, bd)


def _combine_kernel(dest_ref, yg_ref, x1_ref, gate_ref, gt_ref, o_ref, buf_ref, sem):
    i = pl.program_id(0)
    tm = x1_ref.shape[0]

    def issue(g, carry):
        for u in range(ROWS_PER_ISSUE):
            r = g * ROWS_PER_ISSUE + u
            for k in range(TOP_K):
                d = dest_ref[(i * tm + r) * TOP_K + k]
                _row_copy(yg_ref.at[pl.ds(d, 1)], buf_ref.at[k, pl.ds(r, 1)], sem).start()
        return carry

    lax.fori_loop(0, tm // ROWS_PER_ISSUE, issue, 0)

    def drain(g, carry):
        for _ in range(ROWS_PER_ISSUE * TOP_K):
            _row_copy(yg_ref.at[pl.ds(0, 1)], buf_ref.at[0, pl.ds(0, 1)], sem).wait()
        return carry

    lax.fori_loop(0, tm // ROWS_PER_ISSUE, drain, 0)
    gates = gate_ref[...]
    ff = gates[:, 0:1] * buf_ref[0]
    for k in range(1, TOP_K):
        ff = ff + gates[:, k:k + 1] * buf_ref[k]
    o_ref[...] = x1_ref[...] + gt_ref[...] * ff


def _combine(dest_flat, yg, x1, gates, gtx, mod_map):
    t = x1.shape[0]
    tm = TOKEN_TILE
    tok = lambda i, d: (i, 0)
    return pl.pallas_call(
        _combine_kernel,
        grid_spec=pltpu.PrefetchScalarGridSpec(
            num_scalar_prefetch=1,
            grid=(t // tm,),
            in_specs=[pl.BlockSpec(memory_space=pl.ANY),
                      pl.BlockSpec((tm, D_MODEL), tok),
                      pl.BlockSpec((tm, LANES), tok),
                      pl.BlockSpec((tm, D_MODEL), lambda i, d: mod_map(i))],
            out_specs=pl.BlockSpec((tm, D_MODEL), tok),
            scratch_shapes=[pltpu.VMEM((TOP_K, tm, D_MODEL), F32), pltpu.SemaphoreType.DMA(())]),
        out_shape=jax.ShapeDtypeStruct((t, D_MODEL), F32),
        compiler_params=_params(("arbitrary",)),
    )(dest_flat, yg, x1, gates, gtx)


def _route_tables(sel, counts, n_blocks):
    tb = EXPERT_BLOCK
    cnt = counts[0, :N_EXPERTS].astype(jnp.int32)
    padded = (cnt + tb - 1) // tb * tb
    ends = jnp.cumsum(padded)
    starts = ends - padded
    experts = sel[:, :TOP_K]
    ranks = sel[:, TOP_K:2 * TOP_K]
    eid = jnp.arange(N_EXPERTS, dtype=jnp.int32)
    dest = (ranks + jnp.sum(jnp.where(experts[..., None] == eid, starts, 0), axis=-1)).reshape(-1)
    blk_start = jnp.arange(n_blocks, dtype=jnp.int32) * tb
    block_e = jnp.minimum(jnp.sum((ends[None, :] <= blk_start[:, None]).astype(jnp.int32), axis=1), N_EXPERTS - 1)
    n_used = (ends[-1:] // tb).astype(jnp.int32)
    fill_start = jnp.where(cnt % tb != 0, ends - tb, -1)
    return dest.astype(jnp.int32), block_e.astype(jnp.int32), n_used, fill_start.astype(jnp.int32)


def _pad_lanes(v, fill=0.0):
    v = v.reshape(1, -1).astype(F32)
    return jnp.pad(v, ((0, 0), (0, LANES - v.shape[1])), constant_values=fill)


def _pack_w_in(w):
    cuts = np.cumsum([0, POOL_WIDTH, SSD_WIDTH, CONV_DIM, SSD_HEADS, ATT_WIDTH, ATT_WIDTH, ATT_WIDTH])
    pieces = []
    for i, (_, width) in enumerate(_IN_COLS):
        piece = w[:, cuts[i]:cuts[i + 1]]
        pieces.append(jnp.pad(piece, ((0, 0), (0, width - piece.shape[1]))))
    return jnp.concatenate(pieces, axis=1).astype(BF16)


def _block_diag(pool_w):
    g, c, _ = pool_w.shape
    out = jnp.zeros((g * c, g * c), F32)
    for i in range(g):
        out = out.at[i * c:(i + 1) * c, i * c:(i + 1) * c].set(pool_w[i])
    return out.astype(BF16)


def kernel(x_prompt, x_sample, state_pool, state_conv, state_ssm, cache_k, cache_v, c_prompt, c_sample, g_mix, w_ada, b_ada, w_in, pool_w, pool_scale, conv_w, conv_b, dt_bias, a_log, d_skip, ssd_norm_g, w_out, g_ffn, router_w, router_b, w_gate_up, b_gate_up, w_down, b_down, g_final):
    batch, seq, _ = x_prompt.shape
    nseq, steps, _ = x_sample.shape
    depth = w_in.shape[0]
    wbuf = cache_k.shape[2]
    tp = batch * seq
    ts = nseq * steps
    t = tp + ts
    tm = TOKEN_TILE
    assert steps == SUBLANES and seq % (tm * 2) == 0 and ts % tm == 0 and nseq % SEQS_PER_SSD_BLOCK == 0
    assert seq % ATT_TILE == 0

    c_all = jnp.concatenate([c_prompt, c_sample], axis=0)
    pad_rows = (-c_all.shape[0]) % SUBLANES
    c_all = jnp.pad(c_all, ((0, pad_rows), (0, 0)))
    mod = _ada(c_all, w_ada, b_ada)

    tiles_per_seq = seq // tm
    n_prompt_tiles = tp // tm

    def mod_map(i):
        return (jnp.where(i < n_prompt_tiles, i // tiles_per_seq, batch + i - n_prompt_tiles), 0)

    def expand(m):
        mp = jnp.repeat(m[:batch], tm, axis=0)
        ms = jnp.repeat(m[batch:batch + nseq], steps, axis=0)
        return jnp.concatenate([mp, ms], axis=0)

    x = jnp.concatenate([x_prompt.reshape(tp, D_MODEL), x_sample.reshape(ts, D_MODEL)], axis=0)

    n_blocks = -(-(t * TOP_K) // EXPERT_BLOCK) + N_EXPERTS

    pool_p, pool_s, conv_p, conv_s, ssm_p, ssm_s, k_p, v_p = [], [], [], [], [], [], [], []
    kv_s = None
    for l in range(depth):
        sh1, sc1, gt1, sh2, sc2, gt2 = [expand(mod[l, :, i * D_MODEL:(i + 1) * D_MODEL]) for i in range(6)]
        u, z, xbc, dt, q, k, v = _inproj(x, sc1, sh1, g_mix[l].reshape(1, -1), _pack_w_in(w_in[l]), mod_map)

        wbd = _block_diag(pool_w[l])
        pscale = pool_scale[l].reshape(1, -1)
        cw, cb = conv_w[l], conv_b[l].reshape(1, -1)
        dtb, alog, dskip = _pad_lanes(dt_bias[l]), _pad_lanes(a_log[l]), _pad_lanes(d_skip[l])
        ng = ssd_norm_g[l].reshape(1, -1)

        pool_out_p, pool_last = _pool_prompt(u, wbd, pscale, batch, seq)
        ssd_out_p, ssm_new_p = _ssd_prompt(xbc, dt, z, cw, cb, dtb, alog, dskip, ng, batch, seq)
        att_p = _attn_prompt(q, k, v, batch, seq)

        u_s = u[tp:].reshape(nseq, steps, POOL_WIDTH)
        xbc_s = xbc[tp:].reshape(nseq, steps, CONV_DIM)
        tl = lambda a: jnp.swapaxes(a, 0, 1)
        pool_out_s = tl(_pool_sample(tl(state_pool[l]), tl(u_s), wbd, pscale, wbuf)).reshape(ts, POOL_WIDTH)
        xc_s = tl(_conv_sample(tl(state_conv[l]), tl(xbc_s), cw, cb)).reshape(ts, CONV_DIM)
        ssd_out_s, ssm_new_s = _ssd_sample(xc_s, dt, z, state_ssm[l], dtb, alog, dskip, ng, tp, steps)
        att_s, ks_new, vs_new = _attn_sample(q, k, v, cache_k, cache_v, l, tp, steps, kv_s)
        kv_s = (ks_new, vs_new)

        pool_all = (pool_out_p, pool_out_s)
        ssd_all = (ssd_out_p, ssd_out_s)
        att_all = (att_p, att_s.reshape(ts, ATT_WIDTH))

        rw = jnp.pad(router_w[l], ((0, 0), (0, LANES - N_EXPERTS)))
        rb = _pad_lanes(router_b[l], fill=NEG)
        x1, h2, logits = _mixout(pool_all, ssd_all, att_all, x, gt1, sc2, sh2, g_ffn[l].reshape(1, -1),
                                 w_out[l].astype(BF16), rw, rb, mod_map, n_prompt_tiles)
        sel, gates, counts = _router(logits)
        dest, block_e, n_used, fill_start = _route_tables(sel, counts, n_blocks)
        xg = _dispatch(dest, fill_start, n_used, h2, n_blocks)
        yg = _experts(block_e, n_used, xg, w_gate_up, b_gate_up.reshape(depth, N_EXPERTS, 1, -1),
                      w_down, b_down.reshape(depth, N_EXPERTS, 1, -1), l)
        x = _combine(dest, yg, x1, gates, gt2, mod_map)

        pool_p.append(pool_last[:, _POOL_CARRY - POOL_KEEP:])
        pool_s.append(jnp.concatenate([state_pool[l], u_s], axis=1)[:, -POOL_KEEP:])
        conv_p.append(xbc[:tp].reshape(batch, seq, CONV_DIM)[:, -(CONV_W - 1):])
        conv_s.append(jnp.concatenate([state_conv[l], xbc_s], axis=1)[:, -(CONV_W - 1):])
        ssm_p.append(ssm_new_p)
        ssm_s.append(ssm_new_s)
        keep = min(MAX_WINDOW, seq)
        kept = lambda a: jnp.transpose(a[:, :tp].reshape(ATT_CHUNKS, batch, seq, LANES)[:, :, -keep:],
                                       (1, 2, 0, 3)).reshape(batch, keep, ATT_HEADS, HEAD_DIM)
        k_p.append(kept(k))
        v_p.append(kept(v))

    y_prompt = _final_norm(x, g_final.reshape(1, -1), 0, tp).reshape(batch, seq, D_MODEL)
    y_sample = _final_norm(x, g_final.reshape(1, -1), tp, ts).reshape(nseq, steps, D_MODEL)
    pos_major = lambda c: jnp.transpose(c.reshape(depth, nseq, ATT_HEADS, HEAD_DIM, wbuf), (0, 1, 4, 2, 3))
    k_s = pos_major(kv_s[0])
    v_s = pos_major(kv_s[1])
    st = lambda xs: jnp.stack(xs, axis=0)
    return (y_prompt, y_sample, st(pool_p), st(pool_s), st(conv_p), st(conv_s), st(ssm_p), st(ssm_s),
            st(k_p), k_s, st(v_p), v_s)
```

```python
import functools

import numpy as np
import jax
import jax.numpy as jnp
from jax import lax
from jax.experimental import pallas as pl
from jax.experimental.pallas import tpu as pltpu

F32 = jnp.float32
BF16 = jnp.bfloat16

D_MODEL = 1024
HEAD_DIM = 64
POOL_WINDOWS = (2, 4, 8, 16)
POOL_WIDTH = 256
POOL_GROUP = 64
POOL_KEEP = 15
SSD_HEADS = 6
SSD_WIDTH = 384
SSD_GROUPS = 2
HEADS_PER_GROUP = SSD_HEADS // SSD_GROUPS
D_STATE = 64
CONV_W = 4
CONV_DIM = 640
SSD_CHUNK = 128
ATT_HEADS = 6
ATT_WIDTH = 384
DILATED = ((128, 1), (512, 4), (2048, 16))
MAX_WINDOW = 2048
ATT_BLOCK = 128
N_EXPERTS = 32
TOP_K = 4
D_FF = 1024
SWIGLU_LIMIT = 7.0
SWIGLU_ALPHA = 1.702
EPS = 1e-6

LANES = 128
SUBLANES = 8
TOKEN_TILE = 256
EXPERT_BLOCK = 256
SEQS_PER_SSD_BLOCK = 16
VMEM_LIMIT = 48 * 1024 * 1024
EXPERT_VMEM_LIMIT = 56 * 1024 * 1024
NEG = -1e30

_IN_COLS = (("u", POOL_WIDTH), ("z", SSD_WIDTH), ("xbc", CONV_DIM), ("dt", LANES),
            ("q", ATT_WIDTH), ("k", ATT_WIDTH), ("v", ATT_WIDTH))
IN_PACKED = sum(w for _, w in _IN_COLS)
_ATT_NAMES = ("q", "k", "v")
ATT_CHUNKS = ATT_WIDTH // LANES
ATT_TILE = DILATED[-1][1] * ATT_BLOCK


def _params(sem, vmem=VMEM_LIMIT):
    return pltpu.CompilerParams(dimension_semantics=sem, vmem_limit_bytes=vmem)


def _sigmoid(x):
    return 1.0 / (1.0 + jnp.exp(-x))


def _silu(x):
    return x * _sigmoid(x)


def _dot(a, b):
    return jnp.dot(a, b, preferred_element_type=F32)


def _dot_nt(a, b):
    return lax.dot_general(a, b, (((1,), (1,)), ((), ())), preferred_element_type=F32)


def _dot_exact(a, b):
    return jnp.dot(a, b, preferred_element_type=F32, precision=lax.Precision.HIGHEST)


def _rms(x, g):
    ms = jnp.mean(x * x, axis=-1, keepdims=True)
    return x * lax.rsqrt(ms + EPS) * g


def _ada_kernel(c_ref, w_ref, b_ref, o_ref):
    sc = _silu(c_ref[...]).astype(BF16)
    o_ref[0] = _dot(sc, w_ref[0].astype(BF16)) + b_ref[0]


def _ada(c_all, w_ada, b_ada):
    depth, _, n = w_ada.shape
    rows = c_all.shape[0]
    tn = 512
    return pl.pallas_call(
        _ada_kernel,
        grid=(depth, n // tn),
        in_specs=[pl.BlockSpec((rows, D_MODEL), lambda l, j: (0, 0)),
                  pl.BlockSpec((1, D_MODEL, tn), lambda l, j: (l, 0, j)),
                  pl.BlockSpec((1, 1, tn), lambda l, j: (l, 0, j))],
        out_specs=pl.BlockSpec((1, rows, tn), lambda l, j: (l, 0, j)),
        out_shape=jax.ShapeDtypeStruct((depth, rows, n), F32),
        compiler_params=_params(("arbitrary", "arbitrary")),
    )(c_all, w_ada, b_ada.reshape(depth, 1, n))


def _inproj_kernel(x_ref, sc_ref, sh_ref, g_ref, w_ref, u_ref, z_ref, xbc_ref, dt_ref, q_ref, k_ref, v_ref):
    h = _rms(x_ref[...], g_ref[...]) * (1.0 + sc_ref[...]) + sh_ref[...]
    r = _dot(h.astype(BF16), w_ref[...])
    off = 0
    for ref, (name, w) in zip((u_ref, z_ref, xbc_ref, dt_ref, q_ref, k_ref, v_ref), _IN_COLS):
        if name in _ATT_NAMES:
            for c in range(ATT_CHUNKS):
                ref[c] = r[:, off + c * LANES:off + (c + 1) * LANES]
        else:
            ref[...] = r[:, off:off + w]
        off += w


def _inproj(x, scx, shx, g, w_packed, mod_map):
    t = x.shape[0]
    tm = TOKEN_TILE
    tok = lambda i: (i, 0)
    const = lambda i: (0, 0)
    out_specs, out_shape = [], []
    for name, w in _IN_COLS:
        if name in _ATT_NAMES:
            out_specs.append(pl.BlockSpec((ATT_CHUNKS, tm, LANES), lambda i: (0, i, 0)))
            out_shape.append(jax.ShapeDtypeStruct((ATT_CHUNKS, t, LANES), F32))
        else:
            out_specs.append(pl.BlockSpec((tm, w), tok))
            out_shape.append(jax.ShapeDtypeStruct((t, w), F32))
    return pl.pallas_call(
        _inproj_kernel,
        grid=(t // tm,),
        in_specs=[pl.BlockSpec((tm, D_MODEL), tok),
                  pl.BlockSpec((tm, D_MODEL), mod_map),
                  pl.BlockSpec((tm, D_MODEL), mod_map),
                  pl.BlockSpec((1, D_MODEL), const),
                  pl.BlockSpec((D_MODEL, IN_PACKED), const)],
        out_specs=out_specs,
        out_shape=out_shape,
        compiler_params=_params(("arbitrary",)),
    )(x, scx, shx, g, w_packed)


def _split_bf16(a):
    hi = a.astype(BF16)
    lo = (a - hi.astype(F32)).astype(BF16)
    return hi, lo


def _mixout_kernel(pool_p, pool_s, ssd_p, ssd_s, att_p, att_s, x_ref, gt_ref, sc_ref, sh_ref, g_ref, wo_ref,
                   rwh_ref, rwl_ref, rb_ref, x1_ref, h2_ref, lg_ref, *, n_prompt_tiles):
    is_prompt = pl.program_id(0) < n_prompt_tiles
    pick = lambda p_ref, s_ref: jnp.where(is_prompt, p_ref[...], s_ref[...]).astype(BF16)
    mix = _dot(pick(pool_p, pool_s), wo_ref[0:POOL_WIDTH, :])
    mix = mix + _dot(pick(ssd_p, ssd_s), wo_ref[POOL_WIDTH:POOL_WIDTH + SSD_WIDTH, :])
    mix = mix + _dot(pick(att_p, att_s), wo_ref[POOL_WIDTH + SSD_WIDTH:, :])
    x1 = x_ref[...] + gt_ref[...] * mix
    x1_ref[...] = x1
    h2 = _rms(x1, g_ref[...]) * (1.0 + sc_ref[...]) + sh_ref[...]
    _store_rows_as_tiles(h2_ref, h2)
    h_hi, h_lo = _split_bf16(h2)
    lg = _dot(h_hi, rwh_ref[...]) + (_dot(h_hi, rwl_ref[...]) + _dot(h_lo, rwh_ref[...]))
    lg_ref[...] = lg + rb_ref[...]


def _mixout(pool, ssd, att, x, gtx, scx, shx, g, wo, rw, rb, mod_map, n_prompt_tiles):
    t = x.shape[0]
    tm = TOKEN_TILE
    tok = lambda i: (i, 0)
    const = lambda i: (0, 0)
    ptok = lambda i: (jnp.minimum(i, n_prompt_tiles - 1), 0)
    stok = lambda i: (jnp.maximum(i - n_prompt_tiles, 0), 0)
    rw_hi, rw_lo = _split_bf16(rw)
    pair = lambda w: [pl.BlockSpec((tm, w), ptok), pl.BlockSpec((tm, w), stok)]
    return pl.pallas_call(
        functools.partial(_mixout_kernel, n_prompt_tiles=n_prompt_tiles),
        grid=(t // tm,),
        in_specs=pair(POOL_WIDTH) + pair(SSD_WIDTH) + pair(ATT_WIDTH) + [
                  pl.BlockSpec((tm, D_MODEL), tok),
                  pl.BlockSpec((tm, D_MODEL), mod_map),
                  pl.BlockSpec((tm, D_MODEL), mod_map),
                  pl.BlockSpec((tm, D_MODEL), mod_map),
                  pl.BlockSpec((1, D_MODEL), const),
                  pl.BlockSpec((D_MODEL, D_MODEL), const),
                  pl.BlockSpec((D_MODEL, LANES), const),
                  pl.BlockSpec((D_MODEL, LANES), const),
                  pl.BlockSpec((1, LANES), const)],
        out_specs=[pl.BlockSpec((tm, D_MODEL), tok),
                   pl.BlockSpec((tm * ROW_TILES, LANES), tok),
                   pl.BlockSpec((tm, LANES), tok)],
        out_shape=[jax.ShapeDtypeStruct((t, D_MODEL), F32),
                   jax.ShapeDtypeStruct((t * ROW_TILES, LANES), F32),
                   jax.ShapeDtypeStruct((t, LANES), F32)],
        compiler_params=_params(("arbitrary",)),
    )(pool[0], pool[1], ssd[0], ssd[1], att[0], att[1], x, gtx, scx, shx, g, wo, rw_hi, rw_lo, rb)


def _final_norm_kernel(x_ref, g_ref, o_ref):
    o_ref[...] = _rms(x_ref[...], g_ref[...])


def _final_norm(x, g, row0, rows):
    tm = TOKEN_TILE
    blk0 = row0 // tm
    return pl.pallas_call(
        _final_norm_kernel,
        grid=(rows // tm,),
        in_specs=[pl.BlockSpec((tm, D_MODEL), lambda i: (blk0 + i, 0)),
                  pl.BlockSpec((1, D_MODEL), lambda i: (0, 0))],
        out_specs=pl.BlockSpec((tm, D_MODEL), lambda i: (i, 0)),
        out_shape=jax.ShapeDtypeStruct((rows, D_MODEL), F32),
        compiler_params=_params(("arbitrary",)),
    )(x, g)


_POOL_PAD = 8
_POOL_CARRY = 16


def _pool_select(sums, cnts, u):
    lane = lax.broadcasted_iota(jnp.int32, u.shape, u.ndim - 1)
    pooled = sums[-1] / cnts[-1]
    for g in range(len(POOL_WINDOWS) - 2, -1, -1):
        pooled = jnp.where(lane < (g + 1) * POOL_GROUP, sums[g] / cnts[g], pooled)
    return pooled - u


def _pool_prompt_kernel(u_ref, w_ref, scale_ref, o_ref, st_ref, ext_ref, lv_a, lv_b, *, tl):
    j = pl.program_id(1)
    base = _POOL_PAD + _POOL_CARRY
    n = base + tl
    zpad = jnp.zeros((_POOL_PAD, POOL_WIDTH), F32)
    ext_ref[0:_POOL_PAD, :] = zpad
    lv_a[0:_POOL_PAD, :] = zpad
    lv_b[0:_POOL_PAD, :] = zpad

    @pl.when(j == 0)
    def _():
        ext_ref[_POOL_PAD:base, :] = jnp.zeros((_POOL_CARRY, POOL_WIDTH), F32)

    u = u_ref[...]
    ext_ref[base:n, :] = u
    lo = _POOL_PAD
    lv_a[lo:n, :] = ext_ref[lo:n, :] + ext_ref[lo - 1:n - 1, :]
    s2 = lv_a[base:n, :]
    lv_b[lo:n, :] = lv_a[lo:n, :] + lv_a[lo - 2:n - 2, :]
    s4 = lv_b[base:n, :]
    lv_a[lo:n, :] = lv_b[lo:n, :] + lv_b[lo - 4:n - 4, :]
    s8 = lv_a[base:n, :]
    s16 = lv_a[base:n, :] + lv_a[base - 8:n - 8, :]
    pos1 = (lax.broadcasted_iota(jnp.int32, (tl, POOL_WIDTH), 0) + j * tl + 1).astype(F32)
    cnts = [jnp.minimum(pos1, float(w)) for w in POOL_WINDOWS]
    diff = _pool_select([s2, s4, s8, s16], cnts, u)
    o_ref[...] = _dot(diff.astype(BF16), w_ref[...]) * scale_ref[...]
    last = ext_ref[n - _POOL_CARRY:n, :]
    st_ref[0] = last
    ext_ref[_POOL_PAD:base, :] = last


def _pool_prompt(u, wbd, scale, batch, seq):
    tl = 512
    nj = seq // tl
    kern = functools.partial(_pool_prompt_kernel, tl=tl)
    rows = _POOL_PAD + _POOL_CARRY + tl
    return pl.pallas_call(
        kern,
        grid=(batch, nj),
        in_specs=[pl.BlockSpec((tl, POOL_WIDTH), lambda b, j: (b * nj + j, 0)),
                  pl.BlockSpec((POOL_WIDTH, POOL_WIDTH), lambda b, j: (0, 0)),
                  pl.BlockSpec((1, POOL_WIDTH), lambda b, j: (0, 0))],
        out_specs=[pl.BlockSpec((tl, POOL_WIDTH), lambda b, j: (b * nj + j, 0)),
                   pl.BlockSpec((1, _POOL_CARRY, POOL_WIDTH), lambda b, j: (b, 0, 0))],
        out_shape=[jax.ShapeDtypeStruct((batch * seq, POOL_WIDTH), F32),
                   jax.ShapeDtypeStruct((batch, _POOL_CARRY, POOL_WIDTH), F32)],
        scratch_shapes=[pltpu.VMEM((rows, POOL_WIDTH), F32)] * 3,
        compiler_params=_params(("arbitrary", "arbitrary")),
    )(u, wbd, scale)


def _pool_sample_kernel(past_ref, u_ref, w_ref, scale_ref, o_ref, *, n_past, steps, start):
    ext = [past_ref[i] for i in range(n_past)] + [u_ref[t] for t in range(steps)]
    for t in range(steps):
        idx = n_past + t
        sums, cnts = [], []
        acc = ext[idx]
        taken = 1
        for w in POOL_WINDOWS:
            while taken < w:
                if idx - taken >= 0:
                    acc = acc + ext[idx - taken]
                taken += 1
            sums.append(acc)
            cnts.append(float(min(start + t + 1, w)))
        diff = _pool_select(sums, cnts, ext[idx])
        o_ref[t] = _dot(diff.astype(BF16), w_ref[...]) * scale_ref[...]


def _pool_sample(past_t, u_t, wbd, scale, start):
    n_past, bs, _ = past_t.shape
    steps = u_t.shape[0]
    kern = functools.partial(_pool_sample_kernel, n_past=n_past, steps=steps, start=start)
    return pl.pallas_call(
        kern,
        out_shape=jax.ShapeDtypeStruct((steps, bs, POOL_WIDTH), F32),
        compiler_params=pltpu.CompilerParams(vmem_limit_bytes=VMEM_LIMIT),
    )(past_t, u_t, wbd, scale)


def _conv_sample_kernel(past_ref, x_ref, w_ref, b_ref, o_ref, *, steps):
    ext = [past_ref[i] for i in range(CONV_W - 1)] + [x_ref[t] for t in range(steps)]
    for t in range(steps):
        acc = b_ref[...]
        for j in range(CONV_W):
            acc = acc + ext[t + j] * w_ref[j:j + 1, :]
        o_ref[t] = _silu(acc)


def _conv_sample(past_t, x_t, cw, cb):
    steps, bs, _ = x_t.shape
    return pl.pallas_call(
        functools.partial(_conv_sample_kernel, steps=steps),
        out_shape=jax.ShapeDtypeStruct((steps, bs, CONV_DIM), F32),
        compiler_params=pltpu.CompilerParams(vmem_limit_bytes=VMEM_LIMIT),
    )(past_t, x_t, cw, cb)


def _softplus(x):
    return jnp.maximum(x, 0.0) + jnp.log(1.0 + jnp.exp(-jnp.abs(x)))


def _ssd_prepare(xc, dt_raw, dtb, alog, seq_len):
    q = xc.shape[0]
    dt = _softplus(dt_raw + dtb)
    dta = dt * (-jnp.exp(alog))
    ti = lax.broadcasted_iota(jnp.int32, (q, q), 0)
    si = lax.broadcasted_iota(jnp.int32, (q, q), 1)
    same = (ti // seq_len) == (si // seq_len)
    causal = same & (si <= ti)
    after = same & (si > ti)
    acum = _dot_exact(causal.astype(F32), dta)
    suf = _dot_exact(after.astype(F32), dta)
    return dt.T, acum, acum.T, suf.T, causal


def _ssd_diag_head(h, xc, cb, causal, acum, acum_t, dt_t):
    col = acum[:, h:h + 1]
    row = acum_t[h:h + 1, :]
    decay = jnp.where(causal, jnp.exp(jnp.minimum(col - row, 0.0)), 0.0)
    mat = cb * decay * dt_t[h:h + 1, :]
    xh = xc[:, h * HEAD_DIM:(h + 1) * HEAD_DIM]
    return _dot(mat.astype(BF16), xh.astype(BF16)), xh


def _ssd_gate_norm(y, z, ng):
    return _rms(y * _silu(z), ng)


def _ssd_prompt_kernel(xbc_ref, dt_ref, z_ref, cw_ref, cb_ref, dtb_ref, alog_ref, dskip_ref, ng_ref,
                       y_ref, hout_ref, ext_ref, h_ref, ybuf_ref):
    c = pl.program_id(1)
    q = SSD_CHUNK

    @pl.when(c == 0)
    def _():
        ext_ref[0:SUBLANES, :] = jnp.zeros((SUBLANES, CONV_DIM), F32)
        h_ref[...] = jnp.zeros_like(h_ref)

    ext_ref[SUBLANES:SUBLANES + q, :] = xbc_ref[...]
    acc = cb_ref[...]
    for j in range(CONV_W):
        lo = SUBLANES - (CONV_W - 1) + j
        acc = acc + ext_ref[lo:lo + q, :] * cw_ref[j:j + 1, :]
    xc = _silu(acc)
    ext_ref[0:SUBLANES, :] = ext_ref[q:q + SUBLANES, :]

    dt_t, acum, acum_t, suf_t, causal = _ssd_prepare(xc, dt_ref[...], dtb_ref[...], alog_ref[...], q)
    x_t = xc[:, 0:SSD_WIDTH].T
    for g in range(SSD_GROUPS):
        bg = xc[:, SSD_WIDTH + g * D_STATE:SSD_WIDTH + (g + 1) * D_STATE].astype(BF16)
        cg = xc[:, SSD_WIDTH + (SSD_GROUPS + g) * D_STATE:SSD_WIDTH + (SSD_GROUPS + g + 1) * D_STATE].astype(BF16)
        cb = _dot_nt(cg, bg)
        for r in range(HEADS_PER_GROUP):
            h = g * HEADS_PER_GROUP + r
            yd, xh = _ssd_diag_head(h, xc, cb, causal, acum, acum_t, dt_t)
            hst = h_ref[h]
            yo = _dot_nt(cg, hst.astype(BF16)) * jnp.exp(acum[:, h:h + 1])
            ybuf_ref[:, h * HEAD_DIM:(h + 1) * HEAD_DIM] = yd + yo + dskip_ref[0:1, h:h + 1] * xh
            wrow = jnp.exp(suf_t[h:h + 1, :]) * dt_t[h:h + 1, :]
            st = _dot((x_t[h * HEAD_DIM:(h + 1) * HEAD_DIM, :] * wrow).astype(BF16), bg)
            total = acum_t[h:h + 1, q - 1:q]
            h_ref[h] = jnp.exp(total) * hst + st
    y_ref[...] = _ssd_gate_norm(ybuf_ref[...], z_ref[...], ng_ref[...])
    hout_ref[0] = h_ref[...]


def _ssd_prompt(xbc, dt, z, cw, cb, dtb, alog, dskip, ng, batch, seq):
    q = SSD_CHUNK
    nc = seq // q
    tok = lambda b, c: (b * nc + c, 0)
    const = lambda b, c: (0, 0)
    return pl.pallas_call(
        _ssd_prompt_kernel,
        grid=(batch, nc),
        in_specs=[pl.BlockSpec((q, CONV_DIM), tok),
                  pl.BlockSpec((q, LANES), tok),
                  pl.BlockSpec((q, SSD_WIDTH), tok),
                  pl.BlockSpec((CONV_W, CONV_DIM), const),
                  pl.BlockSpec((1, CONV_DIM), const),
                  pl.BlockSpec((1, LANES), const),
                  pl.BlockSpec((1, LANES), const),
                  pl.BlockSpec((1, LANES), const),
                  pl.BlockSpec((1, SSD_WIDTH), const)],
        out_specs=[pl.BlockSpec((q, SSD_WIDTH), tok),
                   pl.BlockSpec((1, SSD_HEADS, HEAD_DIM, D_STATE), lambda b, c: (b, 0, 0, 0))],
        out_shape=[jax.ShapeDtypeStruct((batch * seq, SSD_WIDTH), F32),
                   jax.ShapeDtypeStruct((batch, SSD_HEADS, HEAD_DIM, D_STATE), F32)],
        scratch_shapes=[pltpu.VMEM((q + 2 * SUBLANES, CONV_DIM), F32),
                        pltpu.VMEM((SSD_HEADS, HEAD_DIM, D_STATE), F32),
                        pltpu.VMEM((q, SSD_WIDTH), F32)],
        compiler_params=_params(("arbitrary", "arbitrary")),
    )(xbc, dt, z, cw, cb, dtb, alog, dskip, ng)


def _ssd_sample_kernel(xc_ref, dt_ref, z_ref, h0_ref, dtb_ref, alog_ref, dskip_ref, ng_ref,
                       y_ref, hout_ref, ybuf_ref, acum_ref, xw_ref, *, steps):
    rows = xc_ref.shape[0]
    xc = xc_ref[...]
    dt_t, acum, acum_t, suf_t, causal = _ssd_prepare(xc, dt_ref[...], dtb_ref[...], alog_ref[...], steps)
    acum_ref[...] = acum
    x_t = xc[:, 0:SSD_WIDTH].T
    for g in range(SSD_GROUPS):
        bg = xc[:, SSD_WIDTH + g * D_STATE:SSD_WIDTH + (g + 1) * D_STATE].astype(BF16)
        cg = xc[:, SSD_WIDTH + (SSD_GROUPS + g) * D_STATE:SSD_WIDTH + (SSD_GROUPS + g + 1) * D_STATE].astype(BF16)
        cb = _dot_nt(cg, bg)
        for r in range(HEADS_PER_GROUP):
            h = g * HEADS_PER_GROUP + r
            yd, xh = _ssd_diag_head(h, xc, cb, causal, acum, acum_t, dt_t)
            ybuf_ref[:, h * HEAD_DIM:(h + 1) * HEAD_DIM] = yd + dskip_ref[0:1, h:h + 1] * xh
            wrow = jnp.exp(suf_t[h:h + 1, :]) * dt_t[h:h + 1, :]
            xw_ref[h * HEAD_DIM:(h + 1) * HEAD_DIM, :] = x_t[h * HEAD_DIM:(h + 1) * HEAD_DIM, :] * wrow

    lane = lax.broadcasted_iota(jnp.int32, (1, rows), 1)

    def per_seq(b, carry):
        r0 = pl.multiple_of(b * steps, steps)
        in_seq = (lane // steps) == b
        ac = acum_ref[pl.ds(r0, steps), :]
        for h in range(SSD_HEADS):
            g = h // HEADS_PER_GROUP
            c_lo = SSD_WIDTH + (SSD_GROUPS + g) * D_STATE
            b_lo = SSD_WIDTH + g * D_STATE
            hst = h0_ref[b, h]
            cg = xc_ref[pl.ds(r0, steps), c_lo:c_lo + D_STATE].astype(BF16)
            yo = _dot_nt(cg, hst.astype(BF16)) * jnp.exp(ac[:, h:h + 1])
            sl = slice(h * HEAD_DIM, (h + 1) * HEAD_DIM)
            ybuf_ref[pl.ds(r0, steps), sl] = ybuf_ref[pl.ds(r0, steps), sl] + yo
            xw = jnp.where(in_seq, xw_ref[sl, :], 0.0).astype(BF16)
            st = _dot(xw, xc_ref[:, b_lo:b_lo + D_STATE].astype(BF16))
            total = ac[steps - 1:steps, h:h + 1]
            hout_ref[b, h] = jnp.exp(total) * hst + st
        return carry

    lax.fori_loop(0, rows // steps, per_seq, 0)
    y_ref[...] = _ssd_gate_norm(ybuf_ref[...], z_ref[...], ng_ref[...])


def _ssd_sample(xc, dt, z, h0, dtb, alog, dskip, ng, row0, steps):
    nseq = h0.shape[0]
    nb = SEQS_PER_SSD_BLOCK
    rows = nb * steps
    blk0 = row0 // rows
    const = lambda i: (0, 0)
    return pl.pallas_call(
        functools.partial(_ssd_sample_kernel, steps=steps),
        grid=(nseq // nb,),
        in_specs=[pl.BlockSpec((rows, CONV_DIM), lambda i: (i, 0)),
                  pl.BlockSpec((rows, LANES), lambda i: (blk0 + i, 0)),
                  pl.BlockSpec((rows, SSD_WIDTH), lambda i: (blk0 + i, 0)),
                  pl.BlockSpec((nb, SSD_HEADS, HEAD_DIM, D_STATE), lambda i: (i, 0, 0, 0)),
                  pl.BlockSpec((1, LANES), const),
                  pl.BlockSpec((1, LANES), const),
                  pl.BlockSpec((1, LANES), const),
                  pl.BlockSpec((1, SSD_WIDTH), const)],
        out_specs=[pl.BlockSpec((rows, SSD_WIDTH), lambda i: (i, 0)),
                   pl.BlockSpec((nb, SSD_HEADS, HEAD_DIM, D_STATE), lambda i: (i, 0, 0, 0))],
        out_shape=[jax.ShapeDtypeStruct((nseq * steps, SSD_WIDTH), F32),
                   jax.ShapeDtypeStruct((nseq, SSD_HEADS, HEAD_DIM, D_STATE), F32)],
        scratch_shapes=[pltpu.VMEM((rows, SSD_WIDTH), F32),
                        pltpu.VMEM((rows, LANES), F32),
                        pltpu.VMEM((SSD_WIDTH, rows), F32)],
        compiler_params=_params(("arbitrary",)),
    )(xc, dt, z, h0, dtb, alog, dskip, ng)


def _attn_prompt_kernel(q_ref, k_ref, v_ref, o_ref, kprev, vprev, acc, ms, ls, s_scr, p_scr, m_scr, v_bf):
    i = pl.program_id(1)
    blk = ATT_BLOCK
    tile = ATT_TILE
    scale = HEAD_DIM ** -0.5

    @pl.when(i == 0)
    def _():
        kprev[...] = jnp.zeros_like(kprev)
        vprev[...] = jnp.zeros_like(vprev)

    qi = lax.broadcasted_iota(jnp.int32, (blk, blk), 0)
    kj = lax.broadcasted_iota(jnp.int32, (blk, blk), 1)
    mask_c = kj <= qi
    mask_p_all = kj >= qi
    lane = lax.broadcasted_iota(jnp.int32, (blk, LANES), 1)
    second_head = lane >= HEAD_DIM
    ones = jnp.ones((blk, LANES), BF16)

    nhead = 2 * ATT_CHUNKS

    def run_pattern(d, first):
        span = d * blk
        nfold = tile // span

        def body(idx, carry):
            jb = idx // d
            r = idx % d
            start = jb * span + r
            if d == 1:
                start = pl.multiple_of(start, blk)
            rows = pl.ds(start, blk, stride=d) if d > 1 else pl.ds(start, blk)
            last = tile - span
            rows_prev_tile = pl.ds(last + r, blk, stride=d) if d > 1 else pl.ds(last, blk)
            mask_p = mask_p_all & ((jb > 0) | (i > 0))

            for c in range(ATT_CHUNKS):
                kp = kprev[c, rows_prev_tile, :]
                vp = vprev[c, rows_prev_tile, :]
                if nfold > 1:
                    pstart = jnp.maximum(start - span, r)
                    if d == 1:
                        pstart = pl.multiple_of(pstart, blk)
                    rows_prev = pl.ds(pstart, blk, stride=d) if d > 1 else pl.ds(pstart, blk)
                    kp = jnp.where(jb > 0, k_ref[c, rows_prev, :], kp)
                    vp = jnp.where(jb > 0, v_ref[c, rows_prev, :], vp)
                qc = q_ref[c, rows, :] * scale
                kc = k_ref[c, rows, :].astype(BF16)
                kp = kp.astype(BF16)
                v_bf[2 * c] = v_ref[c, rows, :].astype(BF16)
                v_bf[2 * c + 1] = vp.astype(BF16)
                for hh in range(2):
                    j = 2 * c + hh
                    own = second_head if hh else ~second_head
                    qm = jnp.where(own, qc, 0.0).astype(BF16)
                    s_scr[2 * j] = jnp.where(mask_c, _dot_nt(qm, kc), NEG)
                    s_scr[2 * j + 1] = jnp.where(mask_p, _dot_nt(qm, kp), NEG)
            for j in range(nhead):
                sc = s_scr[2 * j]
                sp = s_scr[2 * j + 1]
                m = jnp.max(jnp.maximum(sc, sp), axis=-1, keepdims=True)
                m_scr[j] = jnp.broadcast_to(m, (blk, LANES))
                p_scr[2 * j] = jnp.exp(sc - m).astype(BF16)
                p_scr[2 * j + 1] = jnp.exp(sp - m).astype(BF16)
            for c in range(ATT_CHUNKS):
                vc = v_bf[2 * c]
                vp = v_bf[2 * c + 1]
                per_head = []
                for hh in range(2):
                    j = 2 * c + hh
                    pc = p_scr[2 * j]
                    pp = p_scr[2 * j + 1]
                    per_head.append((_dot(pc, vc) + _dot(pp, vp), _dot(pc, ones) + _dot(pp, ones), m_scr[j]))
                o, l, m = (jnp.where(second_head, b, a) for a, b in zip(*per_head))
                if first:
                    acc[c, rows, :] = o
                    ms[c, rows, :] = m
                    ls[c, rows, :] = l
                else:
                    m_old = ms[c, rows, :]
                    m_new = jnp.maximum(m_old, m)
                    w_old = jnp.exp(m_old - m_new)
                    w_new = jnp.exp(m - m_new)
                    acc[c, rows, :] = w_old * acc[c, rows, :] + w_new * o
                    ls[c, rows, :] = w_old * ls[c, rows, :] + w_new * l
                    ms[c, rows, :] = m_new
            return carry

        lax.fori_loop(0, tile // blk, body, 0)

    for n, (_, d) in enumerate(DILATED):
        run_pattern(d, n == 0)
    for c in range(ATT_CHUNKS):
        o_ref[:, c * LANES:(c + 1) * LANES] = acc[c] / ls[c]
    kprev[...] = k_ref[...]
    vprev[...] = v_ref[...]


def _attn_prompt(q3, k3, v3, batch, seq):
    tile = ATT_TILE
    nt = seq // tile
    spec = pl.BlockSpec((ATT_CHUNKS, tile, LANES), lambda b, i: (0, b * nt + i, 0))
    scratch = pltpu.VMEM((ATT_CHUNKS, tile, LANES), F32)
    nhead = 2 * ATT_CHUNKS
    return pl.pallas_call(
        _attn_prompt_kernel,
        grid=(batch, nt),
        in_specs=[spec, spec, spec],
        out_specs=pl.BlockSpec((tile, ATT_WIDTH), lambda b, i: (b * nt + i, 0)),
        out_shape=jax.ShapeDtypeStruct((batch * seq, ATT_WIDTH), F32),
        scratch_shapes=[scratch] * 5 + [
            pltpu.VMEM((2 * nhead, ATT_BLOCK, ATT_BLOCK), F32),
            pltpu.VMEM((2 * nhead, ATT_BLOCK, ATT_BLOCK), BF16),
            pltpu.VMEM((nhead, ATT_BLOCK, LANES), F32),
            pltpu.VMEM((2 * ATT_CHUNKS, ATT_BLOCK, LANES), BF16)],
        compiler_params=_params(("arbitrary", "arbitrary")),
    )(q3, k3, v3)


def _sample_multiplicity(steps, wbuf):
    qpos = wbuf + np.arange(steps)[:, None]
    kpos = np.arange(wbuf + steps)[None, :]
    delta = qpos - kpos
    cnt = np.zeros(delta.shape, np.float32)
    for w, d in DILATED:
        cnt += ((delta >= 0) & (delta % d == 0) & (delta // d <= w // d)).astype(np.float32)
    return cnt


def _attn_sample(q, k, v, cache_k, cache_v, layer, row0, steps, prev_out):
    depth, nseq, wbuf = cache_k.shape[0], cache_k.shape[1], cache_k.shape[2]
    chan_major = lambda c: jnp.transpose(c, (0, 1, 3, 4, 2)).reshape(depth, nseq, ATT_WIDTH, wbuf)
    ck = chan_major(cache_k)
    cv = chan_major(cache_v)
    t = q.shape[1]
    view = lambda a: a.reshape(ATT_CHUNKS, t // steps, steps, LANES)
    b0 = row0 // steps
    new = pl.BlockSpec((ATT_CHUNKS, 1, steps, LANES), lambda b: (0, b0 + b, 0, 0))
    window = pl.BlockSpec((1, 1, ATT_WIDTH, wbuf), lambda b: (layer, b, 0, 0))
    cnt = np.tile(_sample_multiplicity(steps, wbuf), (ATT_HEADS, 1))
    cntc = jnp.asarray(cnt[:, :wbuf])
    cntn = jnp.asarray(np.pad(cnt[:, wbuf:], ((0, 0), (0, LANES - steps))))
    rows = ATT_HEADS * steps
    in_specs = [new, new, new, window, window,
                pl.BlockSpec((rows, wbuf), lambda b: (0, 0)),
                pl.BlockSpec((rows, LANES), lambda b: (0, 0))]
    args = [view(q), view(k), view(v), ck, cv, cntc, cntn]
    aliases = {}
    if prev_out is not None:
        in_specs += [pl.BlockSpec(memory_space=pl.ANY)] * 2
        args += list(prev_out)
        aliases = {len(args) - 2: 1, len(args) - 1: 2}
    stacked = jax.ShapeDtypeStruct((depth, nseq, ATT_WIDTH, wbuf), F32)
    return pl.pallas_call(
        functools.partial(_attn_sample_kernel, steps=steps, wbuf=wbuf),
        grid=(nseq,),
        in_specs=in_specs,
        out_specs=[pl.BlockSpec((1, steps, ATT_WIDTH), lambda b: (b, 0, 0)), window, window],
        out_shape=[jax.ShapeDtypeStruct((nseq, steps, ATT_WIDTH), F32), stacked, stacked],
        input_output_aliases=aliases,
        compiler_params=_params(("arbitrary",)),
    )(*args)


def _attn_sample_kernel(*refs, steps, wbuf):
    q_ref, kn_ref, vn_ref, kc_ref, vc_ref, cntc_ref, cntn_ref = refs[:7]
    att_ref, ko_ref, vo_ref = refs[-3:]
    rows = ATT_HEADS * steps
    full = lambda ref: jnp.concatenate([ref[c, 0] for c in range(ATT_CHUNKS)], axis=1)
    q = full(q_ref) * (HEAD_DIM ** -0.5)
    q6 = jnp.concatenate([q] * ATT_HEADS, axis=0)
    row = lax.broadcasted_iota(jnp.int32, (rows, ATT_WIDTH), 0)
    lane = lax.broadcasted_iota(jnp.int32, (rows, ATT_WIDTH), 1)
    own = (lane // HEAD_DIM) == (row // steps)
    qbd = jnp.where(own, q6, 0.0).astype(BF16)
    kc = kc_ref[0, 0]
    vc = vc_ref[0, 0]
    zpad = jnp.zeros((LANES - steps, ATT_WIDTH), F32)
    knp = jnp.concatenate([full(kn_ref), zpad], axis=0)
    vnp = jnp.concatenate([full(vn_ref), zpad], axis=0)
    kn_t = knp.T
    vn_t = vnp.T
    cntc = cntc_ref[...]
    cntn = cntn_ref[...]
    s_c = jnp.where(cntc > 0.0, _dot(qbd, kc.astype(BF16)), NEG)
    s_n = jnp.where(cntn > 0.0, _dot(qbd, kn_t.astype(BF16)), NEG)
    m = jnp.maximum(jnp.max(s_c, axis=-1, keepdims=True), jnp.max(s_n, axis=-1, keepdims=True))
    p_c = cntc * jnp.exp(s_c - m)
    p_n = cntn * jnp.exp(s_n - m)
    l = jnp.sum(p_c, axis=-1, keepdims=True) + jnp.sum(p_n, axis=-1, keepdims=True)
    o = _dot_nt(p_c.astype(BF16), vc.astype(BF16)) + _dot(p_n.astype(BF16), vnp.astype(BF16))
    o = jnp.where(own, o / l, 0.0)
    att_ref[0] = jnp.sum(o.reshape(ATT_HEADS, steps, ATT_WIDTH), axis=0)

    tail = lax.broadcasted_iota(jnp.int32, (ATT_WIDTH, LANES), 1) >= LANES - steps
    for old, new_t, out_ref in ((kc, kn_t, ko_ref), (vc, vn_t, vo_ref)):
        moved = pltpu.roll(old, wbuf - steps, axis=1)
        out_ref[0, 0, :, 0:wbuf - LANES] = moved[:, 0:wbuf - LANES]
        out_ref[0, 0, :, wbuf - LANES:wbuf] = jnp.where(tail, pltpu.roll(new_t, LANES - steps, axis=1),
                                                        moved[:, wbuf - LANES:wbuf])


def _router_kernel(lg_ref, sel_ref, gate_ref, cnt_ref, run_ref):
    i = pl.program_id(0)

    @pl.when(i == 0)
    def _():
        run_ref[...] = jnp.zeros_like(run_ref)

    tm = lg_ref.shape[0]
    work = lg_ref[...]
    lane = lax.broadcasted_iota(jnp.int32, (tm, LANES), 1)
    vals, hots = [], []
    for _ in range(TOP_K):
        m = jnp.max(work, axis=-1, keepdims=True)
        idx = jnp.min(jnp.where(work == m, lane, LANES), axis=-1, keepdims=True)
        hot = lane == idx
        vals.append(m)
        hots.append(hot)
        work = jnp.where(hot, -jnp.inf, work)
    exps = [jnp.exp(v - vals[0]) for v in vals]
    den = exps[0]
    for e in exps[1:]:
        den = den + e
    chosen = hots[0]
    for hot in hots[1:]:
        chosen = chosen | hot
    onehot = chosen.astype(F32)
    ri = lax.broadcasted_iota(jnp.int32, (tm, tm), 0)
    ci = lax.broadcasted_iota(jnp.int32, (tm, tm), 1)
    before = (ci < ri).astype(BF16)
    rank = _dot(before, onehot.astype(BF16)) + run_ref[...]
    sel = jnp.zeros((tm, LANES), jnp.int32)
    gates = jnp.zeros((tm, LANES), F32)
    for k in range(TOP_K):
        e_k = jnp.min(jnp.where(hots[k], lane, LANES), axis=-1, keepdims=True)
        r_k = jnp.sum(jnp.where(hots[k], rank, 0.0), axis=-1, keepdims=True).astype(jnp.int32)
        sel = jnp.where(lane == k, e_k, sel)
        sel = jnp.where(lane == TOP_K + k, r_k, sel)
        gates = jnp.where(lane == k, exps[k] / den, gates)
    sel_ref[...] = sel
    gate_ref[...] = gates
    run_ref[...] = run_ref[...] + jnp.sum(onehot, axis=0, keepdims=True)
    cnt_ref[...] = run_ref[...]


def _router(logits):
    t = logits.shape[0]
    tm = TOKEN_TILE
    tok = lambda i: (i, 0)
    return pl.pallas_call(
        _router_kernel,
        grid=(t // tm,),
        in_specs=[pl.BlockSpec((tm, LANES), tok)],
        out_specs=[pl.BlockSpec((tm, LANES), tok), pl.BlockSpec((tm, LANES), tok),
                   pl.BlockSpec((1, LANES), lambda i: (0, 0))],
        out_shape=[jax.ShapeDtypeStruct((t, LANES), jnp.int32), jax.ShapeDtypeStruct((t, LANES), F32),
                   jax.ShapeDtypeStruct((1, LANES), F32)],
        scratch_shapes=[pltpu.VMEM((1, LANES), F32)],
        compiler_params=_params(("arbitrary",)),
    )(logits)


def _row_copy(src, dst, sem):
    return pltpu.make_async_copy(src, dst, sem)


ROWS_PER_ISSUE = 8

ROW_TILES = D_MODEL // LANES
assert ROW_TILES == SUBLANES


def _store_rows_as_tiles(ref, x):
    n = x.shape[0]
    for s in range(ROW_TILES):
        ref[pl.ds(s, n, stride=ROW_TILES), :] = x[:, s * LANES:(s + 1) * LANES]


def _tile_rows_chunk(ref, s, n, lead=()):
    return ref[lead + (pl.ds(s, n, stride=ROW_TILES), slice(None))]


def _tile_of_row(ref, r, lead=()):
    start = r * ROW_TILES
    if not isinstance(start, int):
        start = pl.multiple_of(start, ROW_TILES)
    return ref.at[lead + (pl.ds(start, ROW_TILES),)]


def _dispatch_kernel(dest_ref, fill_ref, nu_ref, h2_ref, xg_ref, zero_ref, sem, zsem, *, n_blocks):
    i = pl.program_id(0)
    tm = h2_ref.shape[0] // ROW_TILES
    tb = EXPERT_BLOCK

    @pl.when(i == 0)
    def _():
        zero_ref[...] = jnp.zeros_like(zero_ref)

        def fill_copy(start):
            rows = tb * ROW_TILES
            return pltpu.make_async_copy(zero_ref, xg_ref.at[pl.ds(pl.multiple_of(start * ROW_TILES, rows), rows)],
                                         zsem)

        def each_fill(fn):
            def per_expert(e, carry):
                @pl.when(fill_ref[e] >= 0)
                def _():
                    fn(fill_copy(fill_ref[e]))
                return carry

            def per_tail(b, carry):
                fn(fill_copy(b * tb))
                return carry

            lax.fori_loop(0, N_EXPERTS, per_expert, 0)
            lax.fori_loop(nu_ref[0], n_blocks, per_tail, 0)

        each_fill(lambda cp: cp.start())
        each_fill(lambda cp: cp.wait())

    def issue(g, carry):
        for u in range(ROWS_PER_ISSUE):
            r = g * ROWS_PER_ISSUE + u
            for k in range(TOP_K):
                d = dest_ref[(i * tm + r) * TOP_K + k]
                _row_copy(_tile_of_row(h2_ref, r), _tile_of_row(xg_ref, d), sem).start()
        return carry

    lax.fori_loop(0, tm // ROWS_PER_ISSUE, issue, 0)

    def drain(g, carry):
        for _ in range(ROWS_PER_ISSUE * TOP_K):
            _row_copy(_tile_of_row(h2_ref, 0), _tile_of_row(xg_ref, 0), sem).wait()
        return carry

    lax.fori_loop(0, tm // ROWS_PER_ISSUE, drain, 0)


def _dispatch(dest_flat, fill_start, n_used, h2_tiles, n_blocks):
    t = h2_tiles.shape[0] // ROW_TILES
    tm = TOKEN_TILE
    tb = EXPERT_BLOCK
    return pl.pallas_call(
        functools.partial(_dispatch_kernel, n_blocks=n_blocks),
        grid_spec=pltpu.PrefetchScalarGridSpec(
            num_scalar_prefetch=3,
            grid=(t // tm,),
            in_specs=[pl.BlockSpec((tm * ROW_TILES, LANES), lambda i, d, f, n: (i, 0))],
            out_specs=pl.BlockSpec(memory_space=pl.ANY),
            scratch_shapes=[pltpu.VMEM((tb * ROW_TILES, LANES), F32), pltpu.SemaphoreType.DMA(()),
                            pltpu.SemaphoreType.DMA(())]),
        out_shape=jax.ShapeDtypeStruct((n_blocks * tb * ROW_TILES, LANES), F32),
        compiler_params=_params(("arbitrary",)),
    )(dest_flat, fill_start, n_used, h2_tiles)


_CAST_ROWS = 128


def _expert_kernel(be_ref, nu_ref, x_ref, wgu_ref, bgu_ref, wd_ref, bd_ref, y_ref, wgu_bf, wd_bf, x_bf):
    i = pl.program_id(0)
    tb = EXPERT_BLOCK

    @pl.when(i < nu_ref[0])
    def _():
        @pl.when((i == 0) | (be_ref[i] != be_ref[jnp.maximum(i - 1, 0)]))
        def _():
            def cast(c, carry):
                r = pl.multiple_of(c * _CAST_ROWS, _CAST_ROWS)
                wgu_bf[pl.ds(r, _CAST_ROWS), :] = wgu_ref[0, 0, pl.ds(r, _CAST_ROWS), :].astype(BF16)
                wd_bf[pl.ds(r, _CAST_ROWS), :] = wd_ref[0, 0, pl.ds(r, _CAST_ROWS), :].astype(BF16)
                return carry

            lax.fori_loop(0, D_MODEL // _CAST_ROWS, cast, 0)

        for s in range(ROW_TILES):
            x_bf[:, s * LANES:(s + 1) * LANES] = _tile_rows_chunk(x_ref, s, tb).astype(BF16)
        gu = _dot(x_bf[...], wgu_bf[...]) + bgu_ref[0, 0]
        gate = jnp.minimum(gu[:, :D_FF], SWIGLU_LIMIT)
        up = jnp.clip(gu[:, D_FF:], -SWIGLU_LIMIT, SWIGLU_LIMIT)
        act = (up + 1.0) * gate * _sigmoid(SWIGLU_ALPHA * gate)
        _store_rows_as_tiles(y_ref, _dot(act.astype(BF16), wd_bf[...]) + bd_ref[0, 0])

    @pl.when(i >= nu_ref[0])
    def _():
        y_ref[...] = jnp.zeros_like(y_ref)


def _experts(block_e, n_used, xg, wgu, bgu, wd, bd, layer):
    assert D_FF == D_MODEL
    n_rows = xg.shape[0] // ROW_TILES
    tb = EXPERT_BLOCK
    blk = lambda i, be, nu: (jnp.minimum(i, nu[0] - 1), 0)
    exp4 = lambda i, be, nu: (layer, be[i], 0, 0)
    return pl.pallas_call(
        _expert_kernel,
        grid_spec=pltpu.PrefetchScalarGridSpec(
            num_scalar_prefetch=2,
            grid=(n_rows // tb,),
            in_specs=[pl.BlockSpec((tb * ROW_TILES, LANES), blk),
                      pl.BlockSpec((1, 1, D_MODEL, 2 * D_FF), exp4),
                      pl.BlockSpec((1, 1, 1, 2 * D_FF), exp4),
                      pl.BlockSpec((1, 1, D_FF, D_MODEL), exp4),
                      pl.BlockSpec((1, 1, 1, D_MODEL), exp4)],
            out_specs=pl.BlockSpec((tb * ROW_TILES, LANES), lambda i, be, nu: (i, 0)),
            scratch_shapes=[pltpu.VMEM((D_MODEL, 2 * D_FF), BF16), pltpu.VMEM((D_FF, D_MODEL), BF16),
                            pltpu.VMEM((tb, D_MODEL), BF16)]),
        out_shape=jax.ShapeDtypeStruct((n_rows * ROW_TILES, LANES), F32),
        compiler_params=_params(("arbitrary",), vmem=EXPERT_VMEM_LIMIT),
    )(block_e, n_used, xg, wgu, bgu, wd, bd)


def _combine_kernel(dest_ref, yg_ref, x1_ref, gate_ref, gt_ref, o_ref, buf_ref, sem):
    i = pl.program_id(0)
    tm = x1_ref.shape[0]

    def issue(g, carry):
        for u in range(ROWS_PER_ISSUE):
            r = g * ROWS_PER_ISSUE + u
            for k in range(TOP_K):
                d = dest_ref[(i * tm + r) * TOP_K + k]
                _row_copy(_tile_of_row(yg_ref, d), _tile_of_row(buf_ref, r, (k,)), sem).start()
        return carry

    lax.fori_loop(0, tm // ROWS_PER_ISSUE, issue, 0)

    def drain(g, carry):
        for _ in range(ROWS_PER_ISSUE * TOP_K):
            _row_copy(_tile_of_row(yg_ref, 0), _tile_of_row(buf_ref, 0, (0,)), sem).wait()
        return carry

    lax.fori_loop(0, tm // ROWS_PER_ISSUE, drain, 0)
    gates = gate_ref[...]
    for s in range(ROW_TILES):
        ff = gates[:, 0:1] * _tile_rows_chunk(buf_ref, s, tm, (0,))
        for k in range(1, TOP_K):
            ff = ff + gates[:, k:k + 1] * _tile_rows_chunk(buf_ref, s, tm, (k,))
        sl = slice(s * LANES, (s + 1) * LANES)
        o_ref[:, sl] = x1_ref[:, sl] + gt_ref[:, sl] * ff


def _combine(dest_flat, yg, x1, gates, gtx, mod_map):
    t = x1.shape[0]
    tm = TOKEN_TILE
    tok = lambda i, d: (i, 0)
    return pl.pallas_call(
        _combine_kernel,
        grid_spec=pltpu.PrefetchScalarGridSpec(
            num_scalar_prefetch=1,
            grid=(t // tm,),
            in_specs=[pl.BlockSpec(memory_space=pl.ANY),
                      pl.BlockSpec((tm, D_MODEL), tok),
                      pl.BlockSpec((tm, LANES), tok),
                      pl.BlockSpec((tm, D_MODEL), lambda i, d: mod_map(i))],
            out_specs=pl.BlockSpec((tm, D_MODEL), tok),
            scratch_shapes=[pltpu.VMEM((TOP_K, tm * ROW_TILES, LANES), F32), pltpu.SemaphoreType.DMA(())]),
        out_shape=jax.ShapeDtypeStruct((t, D_MODEL), F32),
        compiler_params=_params(("arbitrary",)),
    )(dest_flat, yg, x1, gates, gtx)


def _route_tables(sel, counts, n_blocks):
    tb = EXPERT_BLOCK
    cnt = counts[0, :N_EXPERTS].astype(jnp.int32)
    padded = (cnt + tb - 1) // tb * tb
    ends = jnp.cumsum(padded)
    starts = ends - padded
    experts = sel[:, :TOP_K]
    ranks = sel[:, TOP_K:2 * TOP_K]
    eid = jnp.arange(N_EXPERTS, dtype=jnp.int32)
    dest = (ranks + jnp.sum(jnp.where(experts[..., None] == eid, starts, 0), axis=-1)).reshape(-1)
    blk_start = jnp.arange(n_blocks, dtype=jnp.int32) * tb
    block_e = jnp.minimum(jnp.sum((ends[None, :] <= blk_start[:, None]).astype(jnp.int32), axis=1), N_EXPERTS - 1)
    n_used = (ends[-1:] // tb).astype(jnp.int32)
    fill_start = jnp.where(cnt % tb != 0, ends - tb, -1)
    return dest.astype(jnp.int32), block_e.astype(jnp.int32), n_used, fill_start.astype(jnp.int32)


def _pad_lanes(v, fill=0.0):
    v = v.reshape(1, -1).astype(F32)
    return jnp.pad(v, ((0, 0), (0, LANES - v.shape[1])), constant_values=fill)


def _pack_w_in(w):
    cuts = np.cumsum([0, POOL_WIDTH, SSD_WIDTH, CONV_DIM, SSD_HEADS, ATT_WIDTH, ATT_WIDTH, ATT_WIDTH])
    pieces = []
    for i, (_, width) in enumerate(_IN_COLS):
        piece = w[:, cuts[i]:cuts[i + 1]]
        pieces.append(jnp.pad(piece, ((0, 0), (0, width - piece.shape[1]))))
    return jnp.concatenate(pieces, axis=1).astype(BF16)


def _block_diag(pool_w):
    g, c, _ = pool_w.shape
    out = jnp.zeros((g * c, g * c), F32)
    for i in range(g):
        out = out.at[i * c:(i + 1) * c, i * c:(i + 1) * c].set(pool_w[i])
    return out.astype(BF16)


def kernel(x_prompt, x_sample, state_pool, state_conv, state_ssm, cache_k, cache_v, c_prompt, c_sample, g_mix, w_ada, b_ada, w_in, pool_w, pool_scale, conv_w, conv_b, dt_bias, a_log, d_skip, ssd_norm_g, w_out, g_ffn, router_w, router_b, w_gate_up, b_gate_up, w_down, b_down, g_final):
    batch, seq, _ = x_prompt.shape
    nseq, steps, _ = x_sample.shape
    depth = w_in.shape[0]
    wbuf = cache_k.shape[2]
    tp = batch * seq
    ts = nseq * steps
    t = tp + ts
    tm = TOKEN_TILE
    assert steps == SUBLANES and seq % (tm * 2) == 0 and ts % tm == 0 and nseq % SEQS_PER_SSD_BLOCK == 0
    assert seq % ATT_TILE == 0

    c_all = jnp.concatenate([c_prompt, c_sample], axis=0)
    pad_rows = (-c_all.shape[0]) % SUBLANES
    c_all = jnp.pad(c_all, ((0, pad_rows), (0, 0)))
    mod = _ada(c_all, w_ada, b_ada)

    tiles_per_seq = seq // tm
    n_prompt_tiles = tp // tm

    def mod_map(i):
        return (jnp.where(i < n_prompt_tiles, i // tiles_per_seq, batch + i - n_prompt_tiles), 0)

    def expand(m):
        mp = jnp.repeat(m[:batch], tm, axis=0)
        ms = jnp.repeat(m[batch:batch + nseq], steps, axis=0)
        return jnp.concatenate([mp, ms], axis=0)

    x = jnp.concatenate([x_prompt.reshape(tp, D_MODEL), x_sample.reshape(ts, D_MODEL)], axis=0)

    n_blocks = -(-(t * TOP_K) // EXPERT_BLOCK) + N_EXPERTS

    pool_p, pool_s, conv_p, conv_s, ssm_p, ssm_s, k_p, v_p = [], [], [], [], [], [], [], []
    kv_s = None
    for l in range(depth):
        sh1, sc1, gt1, sh2, sc2, gt2 = [expand(mod[l, :, i * D_MODEL:(i + 1) * D_MODEL]) for i in range(6)]
        u, z, xbc, dt, q, k, v = _inproj(x, sc1, sh1, g_mix[l].reshape(1, -1), _pack_w_in(w_in[l]), mod_map)

        wbd = _block_diag(pool_w[l])
        pscale = pool_scale[l].reshape(1, -1)
        cw, cb = conv_w[l], conv_b[l].reshape(1, -1)
        dtb, alog, dskip = _pad_lanes(dt_bias[l]), _pad_lanes(a_log[l]), _pad_lanes(d_skip[l])
        ng = ssd_norm_g[l].reshape(1, -1)

        pool_out_p, pool_last = _pool_prompt(u, wbd, pscale, batch, seq)
        ssd_out_p, ssm_new_p = _ssd_prompt(xbc, dt, z, cw, cb, dtb, alog, dskip, ng, batch, seq)
        att_p = _attn_prompt(q, k, v, batch, seq)

        u_s = u[tp:].reshape(nseq, steps, POOL_WIDTH)
        xbc_s = xbc[tp:].reshape(nseq, steps, CONV_DIM)
        tl = lambda a: jnp.swapaxes(a, 0, 1)
        pool_out_s = tl(_pool_sample(tl(state_pool[l]), tl(u_s), wbd, pscale, wbuf)).reshape(ts, POOL_WIDTH)
        xc_s = tl(_conv_sample(tl(state_conv[l]), tl(xbc_s), cw, cb)).reshape(ts, CONV_DIM)
        ssd_out_s, ssm_new_s = _ssd_sample(xc_s, dt, z, state_ssm[l], dtb, alog, dskip, ng, tp, steps)
        att_s, ks_new, vs_new = _attn_sample(q, k, v, cache_k, cache_v, l, tp, steps, kv_s)
        kv_s = (ks_new, vs_new)

        pool_all = (pool_out_p, pool_out_s)
        ssd_all = (ssd_out_p, ssd_out_s)
        att_all = (att_p, att_s.reshape(ts, ATT_WIDTH))

        rw = jnp.pad(router_w[l], ((0, 0), (0, LANES - N_EXPERTS)))
        rb = _pad_lanes(router_b[l], fill=NEG)
        x1, h2, logits = _mixout(pool_all, ssd_all, att_all, x, gt1, sc2, sh2, g_ffn[l].reshape(1, -1),
                                 w_out[l].astype(BF16), rw, rb, mod_map, n_prompt_tiles)
        sel, gates, counts = _router(logits)
        dest, block_e, n_used, fill_start = _route_tables(sel, counts, n_blocks)
        xg = _dispatch(dest, fill_start, n_used, h2, n_blocks)
        yg = _experts(block_e, n_used, xg, w_gate_up, b_gate_up.reshape(depth, N_EXPERTS, 1, -1),
                      w_down, b_down.reshape(depth, N_EXPERTS, 1, -1), l)
        x = _combine(dest, yg, x1, gates, gt2, mod_map)

        pool_p.append(pool_last[:, _POOL_CARRY - POOL_KEEP:])
        pool_s.append(jnp.concatenate([state_pool[l], u_s], axis=1)[:, -POOL_KEEP:])
        conv_p.append(jnp.stack([xbc[(b + 1) * seq - (CONV_W - 1):(b + 1) * seq] for b in range(batch)], axis=0))
        conv_s.append(jnp.concatenate([state_conv[l], xbc_s], axis=1)[:, -(CONV_W - 1):])
        ssm_p.append(ssm_new_p)
        ssm_s.append(ssm_new_s)
        keep = min(MAX_WINDOW, seq)
        kept = lambda a: jnp.transpose(
            jnp.stack([a[:, (b + 1) * seq - keep:(b + 1) * seq] for b in range(batch)], axis=0),
            (0, 2, 1, 3)).reshape(batch, keep, ATT_HEADS, HEAD_DIM)
        k_p.append(kept(k))
        v_p.append(kept(v))

    y_prompt = _final_norm(x, g_final.reshape(1, -1), 0, tp).reshape(batch, seq, D_MODEL)
    y_sample = _final_norm(x, g_final.reshape(1, -1), tp, ts).reshape(nseq, steps, D_MODEL)
    pos_major = lambda c: jnp.transpose(c.reshape(depth, nseq, ATT_HEADS, HEAD_DIM, wbuf), (0, 1, 4, 2, 3))
    k_s = pos_major(kv_s[0])
    v_s = pos_major(kv_s[1])
    st = lambda xs: jnp.stack(xs, axis=0)
    return (y_prompt, y_sample, st(pool_p), st(pool_s), st(conv_p), st(conv_s), st(ssm_p), st(ssm_s),
            st(k_p), k_s, st(v_p), v_s)
```

```python
import functools

import numpy as np
import jax
import jax.numpy as jnp
from jax import lax
from jax.experimental import pallas as pl
from jax.experimental.pallas import tpu as pltpu

F32 = jnp.float32
BF16 = jnp.bfloat16

D_MODEL = 1024
HEAD_DIM = 64
POOL_WINDOWS = (2, 4, 8, 16)
POOL_WIDTH = 256
POOL_GROUP = 64
POOL_KEEP = 15
SSD_HEADS = 6
SSD_WIDTH = 384
SSD_GROUPS = 2
HEADS_PER_GROUP = SSD_HEADS // SSD_GROUPS
D_STATE = 64
CONV_W = 4
CONV_DIM = 640
SSD_CHUNK = 128
ATT_HEADS = 6
ATT_WIDTH = 384
DILATED = ((128, 1), (512, 4), (2048, 16))
MAX_WINDOW = 2048
ATT_BLOCK = 128
N_EXPERTS = 32
TOP_K = 4
D_FF = 1024
SWIGLU_LIMIT = 7.0
SWIGLU_ALPHA = 1.702
EPS = 1e-6

LANES = 128
SUBLANES = 8
TOKEN_TILE = 256
EXPERT_BLOCK = 256
SEQS_PER_SSD_BLOCK = 16
VMEM_LIMIT = 48 * 1024 * 1024
EXPERT_VMEM_LIMIT = 56 * 1024 * 1024
NEG = -1e30

_IN_COLS = (("u", POOL_WIDTH), ("z", SSD_WIDTH), ("xbc", CONV_DIM), ("dt", LANES),
            ("q", ATT_WIDTH), ("k", ATT_WIDTH), ("v", ATT_WIDTH))
IN_PACKED = sum(w for _, w in _IN_COLS)
MOD_SH1, MOD_SC1, MOD_GT1, MOD_SH2, MOD_SC2, MOD_GT2 = range(6)
_ATT_NAMES = ("q", "k", "v")
ATT_CHUNKS = ATT_WIDTH // LANES
ATT_TILE = DILATED[-1][1] * ATT_BLOCK


def _params(sem, vmem=VMEM_LIMIT):
    return pltpu.CompilerParams(dimension_semantics=sem, vmem_limit_bytes=vmem)


def _sigmoid(x):
    return 1.0 / (1.0 + jnp.exp(-x))


def _silu(x):
    return x * _sigmoid(x)


def _dot(a, b):
    return jnp.dot(a, b, preferred_element_type=F32)


def _dot_nt(a, b):
    return lax.dot_general(a, b, (((1,), (1,)), ((), ())), preferred_element_type=F32)


def _dot_exact(a, b):
    return jnp.dot(a, b, preferred_element_type=F32, precision=lax.Precision.HIGHEST)


def _rms(x, g):
    ms = jnp.mean(x * x, axis=-1, keepdims=True)
    return x * lax.rsqrt(ms + EPS) * g


def _ada_kernel(c_ref, w_ref, b_ref, o_ref):
    sc = _silu(c_ref[...]).astype(BF16)
    o_ref[0] = _dot(sc, w_ref[0].astype(BF16)) + b_ref[0]


def _ada(c_all, w_ada, b_ada):
    depth, _, n = w_ada.shape
    rows = c_all.shape[0]
    tn = 512
    return pl.pallas_call(
        _ada_kernel,
        grid=(depth, n // tn),
        in_specs=[pl.BlockSpec((rows, D_MODEL), lambda l, j: (0, 0)),
                  pl.BlockSpec((1, D_MODEL, tn), lambda l, j: (l, 0, j)),
                  pl.BlockSpec((1, 1, tn), lambda l, j: (l, 0, j))],
        out_specs=pl.BlockSpec((1, rows, tn), lambda l, j: (l, 0, j)),
        out_shape=jax.ShapeDtypeStruct((depth, rows, n), F32),
        compiler_params=_params(("arbitrary", "arbitrary")),
    )(c_all, w_ada, b_ada.reshape(depth, 1, n))


def _inproj_kernel(x_ref, sc_ref, sh_ref, g_ref, w_ref, u_ref, z_ref, xbc_ref, dt_ref, q_ref, k_ref, v_ref):
    h = _rms(x_ref[...], g_ref[...]) * (1.0 + sc_ref[...]) + sh_ref[...]
    r = _dot(h.astype(BF16), w_ref[...])
    off = 0
    for ref, (name, w) in zip((u_ref, z_ref, xbc_ref, dt_ref, q_ref, k_ref, v_ref), _IN_COLS):
        if name in _ATT_NAMES:
            for c in range(ATT_CHUNKS):
                ref[c] = r[:, off + c * LANES:off + (c + 1) * LANES]
        else:
            ref[...] = r[:, off:off + w]
        off += w


def _inproj(x, scx, shx, g, w_packed, mod_map):
    t = x.shape[0]
    tm = TOKEN_TILE
    tok = lambda i: (i, 0)
    const = lambda i: (0, 0)
    out_specs, out_shape = [], []
    for name, w in _IN_COLS:
        if name in _ATT_NAMES:
            out_specs.append(pl.BlockSpec((ATT_CHUNKS, tm, LANES), lambda i: (0, i, 0)))
            out_shape.append(jax.ShapeDtypeStruct((ATT_CHUNKS, t, LANES), F32))
        else:
            out_specs.append(pl.BlockSpec((tm, w), tok))
            out_shape.append(jax.ShapeDtypeStruct((t, w), F32))
    return pl.pallas_call(
        _inproj_kernel,
        grid=(t // tm,),
        in_specs=[pl.BlockSpec((tm, D_MODEL), tok),
                  pl.BlockSpec((None, tm, D_MODEL), mod_map(MOD_SC1)),
                  pl.BlockSpec((None, tm, D_MODEL), mod_map(MOD_SH1)),
                  pl.BlockSpec((1, D_MODEL), const),
                  pl.BlockSpec((D_MODEL, IN_PACKED), const)],
        out_specs=out_specs,
        out_shape=out_shape,
        compiler_params=_params(("arbitrary",)),
    )(x, scx, shx, g, w_packed)


def _split_bf16(a):
    hi = a.astype(BF16)
    lo = (a - hi.astype(F32)).astype(BF16)
    return hi, lo


def _mixout_kernel(pool_p, pool_s, ssd_p, ssd_s, att_p, att_s, x_ref, gt_ref, sc_ref, sh_ref, g_ref, wo_ref,
                   rwh_ref, rwl_ref, rb_ref, x1_ref, h2_ref, lg_ref, *, n_prompt_tiles):
    is_prompt = pl.program_id(0) < n_prompt_tiles
    pick = lambda p_ref, s_ref: jnp.where(is_prompt, p_ref[...], s_ref[...]).astype(BF16)
    mix = _dot(pick(pool_p, pool_s), wo_ref[0:POOL_WIDTH, :])
    mix = mix + _dot(pick(ssd_p, ssd_s), wo_ref[POOL_WIDTH:POOL_WIDTH + SSD_WIDTH, :])
    mix = mix + _dot(pick(att_p, att_s), wo_ref[POOL_WIDTH + SSD_WIDTH:, :])
    x1 = x_ref[...] + gt_ref[...] * mix
    x1_ref[...] = x1
    h2 = _rms(x1, g_ref[...]) * (1.0 + sc_ref[...]) + sh_ref[...]
    _store_rows_as_tiles(h2_ref, h2)
    h_hi, h_lo = _split_bf16(h2)
    lg = _dot(h_hi, rwh_ref[...]) + (_dot(h_hi, rwl_ref[...]) + _dot(h_lo, rwh_ref[...]))
    lg_ref[...] = lg + rb_ref[...]


def _mixout(pool, ssd, att, x, gtx, scx, shx, g, wo, rw, rb, mod_map, n_prompt_tiles):
    t = x.shape[0]
    tm = TOKEN_TILE
    tok = lambda i: (i, 0)
    const = lambda i: (0, 0)
    ptok = lambda i: (jnp.minimum(i, n_prompt_tiles - 1), 0)
    stok = lambda i: (jnp.maximum(i - n_prompt_tiles, 0), 0)
    rw_hi, rw_lo = _split_bf16(rw)
    pair = lambda w: [pl.BlockSpec((tm, w), ptok), pl.BlockSpec((tm, w), stok)]
    return pl.pallas_call(
        functools.partial(_mixout_kernel, n_prompt_tiles=n_prompt_tiles),
        grid=(t // tm,),
        in_specs=pair(POOL_WIDTH) + pair(SSD_WIDTH) + pair(ATT_WIDTH) + [
                  pl.BlockSpec((tm, D_MODEL), tok),
                  pl.BlockSpec((None, tm, D_MODEL), mod_map(MOD_GT1)),
                  pl.BlockSpec((None, tm, D_MODEL), mod_map(MOD_SC2)),
                  pl.BlockSpec((None, tm, D_MODEL), mod_map(MOD_SH2)),
                  pl.BlockSpec((1, D_MODEL), const),
                  pl.BlockSpec((D_MODEL, D_MODEL), const),
                  pl.BlockSpec((D_MODEL, LANES), const),
                  pl.BlockSpec((D_MODEL, LANES), const),
                  pl.BlockSpec((1, LANES), const)],
        out_specs=[pl.BlockSpec((tm, D_MODEL), tok),
                   pl.BlockSpec((tm * ROW_TILES, LANES), tok),
                   pl.BlockSpec((tm, LANES), tok)],
        out_shape=[jax.ShapeDtypeStruct((t, D_MODEL), F32),
                   jax.ShapeDtypeStruct((t * ROW_TILES, LANES), F32),
                   jax.ShapeDtypeStruct((t, LANES), F32)],
        compiler_params=_params(("arbitrary",)),
    )(pool[0], pool[1], ssd[0], ssd[1], att[0], att[1], x, gtx, scx, shx, g, wo, rw_hi, rw_lo, rb)


def _final_norm_kernel(x_ref, g_ref, o_ref):
    o_ref[...] = _rms(x_ref[...], g_ref[...])


def _final_norm(x, g, row0, rows):
    tm = TOKEN_TILE
    blk0 = row0 // tm
    return pl.pallas_call(
        _final_norm_kernel,
        grid=(rows // tm,),
        in_specs=[pl.BlockSpec((tm, D_MODEL), lambda i: (blk0 + i, 0)),
                  pl.BlockSpec((1, D_MODEL), lambda i: (0, 0))],
        out_specs=pl.BlockSpec((tm, D_MODEL), lambda i: (i, 0)),
        out_shape=jax.ShapeDtypeStruct((rows, D_MODEL), F32),
        compiler_params=_params(("arbitrary",)),
    )(x, g)


_POOL_PAD = 8
_POOL_CARRY = 16


def _pool_select(sums, cnts, u):
    lane = lax.broadcasted_iota(jnp.int32, u.shape, u.ndim - 1)
    pooled = sums[-1] / cnts[-1]
    for g in range(len(POOL_WINDOWS) - 2, -1, -1):
        pooled = jnp.where(lane < (g + 1) * POOL_GROUP, sums[g] / cnts[g], pooled)
    return pooled - u


def _pool_prompt_kernel(u_ref, w_ref, scale_ref, o_ref, st_ref, ext_ref, lv_a, lv_b, *, tl):
    j = pl.program_id(1)
    base = _POOL_PAD + _POOL_CARRY
    n = base + tl
    zpad = jnp.zeros((_POOL_PAD, POOL_WIDTH), F32)
    ext_ref[0:_POOL_PAD, :] = zpad
    lv_a[0:_POOL_PAD, :] = zpad
    lv_b[0:_POOL_PAD, :] = zpad

    @pl.when(j == 0)
    def _():
        ext_ref[_POOL_PAD:base, :] = jnp.zeros((_POOL_CARRY, POOL_WIDTH), F32)

    u = u_ref[...]
    ext_ref[base:n, :] = u
    lo = _POOL_PAD
    lv_a[lo:n, :] = ext_ref[lo:n, :] + ext_ref[lo - 1:n - 1, :]
    s2 = lv_a[base:n, :]
    lv_b[lo:n, :] = lv_a[lo:n, :] + lv_a[lo - 2:n - 2, :]
    s4 = lv_b[base:n, :]
    lv_a[lo:n, :] = lv_b[lo:n, :] + lv_b[lo - 4:n - 4, :]
    s8 = lv_a[base:n, :]
    s16 = lv_a[base:n, :] + lv_a[base - 8:n - 8, :]
    pos1 = (lax.broadcasted_iota(jnp.int32, (tl, POOL_WIDTH), 0) + j * tl + 1).astype(F32)
    cnts = [jnp.minimum(pos1, float(w)) for w in POOL_WINDOWS]
    diff = _pool_select([s2, s4, s8, s16], cnts, u)
    o_ref[...] = _dot(diff.astype(BF16), w_ref[...]) * scale_ref[...]
    last = ext_ref[n - _POOL_CARRY:n, :]
    st_ref[0] = last
    ext_ref[_POOL_PAD:base, :] = last


def _pool_prompt(u, wbd, scale, batch, seq):
    tl = 512
    nj = seq // tl
    kern = functools.partial(_pool_prompt_kernel, tl=tl)
    rows = _POOL_PAD + _POOL_CARRY + tl
    return pl.pallas_call(
        kern,
        grid=(batch, nj),
        in_specs=[pl.BlockSpec((tl, POOL_WIDTH), lambda b, j: (b * nj + j, 0)),
                  pl.BlockSpec((POOL_WIDTH, POOL_WIDTH), lambda b, j: (0, 0)),
                  pl.BlockSpec((1, POOL_WIDTH), lambda b, j: (0, 0))],
        out_specs=[pl.BlockSpec((tl, POOL_WIDTH), lambda b, j: (b * nj + j, 0)),
                   pl.BlockSpec((1, _POOL_CARRY, POOL_WIDTH), lambda b, j: (b, 0, 0))],
        out_shape=[jax.ShapeDtypeStruct((batch * seq, POOL_WIDTH), F32),
                   jax.ShapeDtypeStruct((batch, _POOL_CARRY, POOL_WIDTH), F32)],
        scratch_shapes=[pltpu.VMEM((rows, POOL_WIDTH), F32)] * 3,
        compiler_params=_params(("arbitrary", "arbitrary")),
    )(u, wbd, scale)


def _pool_sample_kernel(past_ref, u_ref, w_ref, scale_ref, o_ref, *, n_past, steps, start):
    ext = [past_ref[i] for i in range(n_past)] + [u_ref[t] for t in range(steps)]
    for t in range(steps):
        idx = n_past + t
        sums, cnts = [], []
        acc = ext[idx]
        taken = 1
        for w in POOL_WINDOWS:
            while taken < w:
                if idx - taken >= 0:
                    acc = acc + ext[idx - taken]
                taken += 1
            sums.append(acc)
            cnts.append(float(min(start + t + 1, w)))
        diff = _pool_select(sums, cnts, ext[idx])
        o_ref[t] = _dot(diff.astype(BF16), w_ref[...]) * scale_ref[...]


def _pool_sample(past_t, u_t, wbd, scale, start):
    n_past, bs, _ = past_t.shape
    steps = u_t.shape[0]
    kern = functools.partial(_pool_sample_kernel, n_past=n_past, steps=steps, start=start)
    return pl.pallas_call(
        kern,
        out_shape=jax.ShapeDtypeStruct((steps, bs, POOL_WIDTH), F32),
        compiler_params=pltpu.CompilerParams(vmem_limit_bytes=VMEM_LIMIT),
    )(past_t, u_t, wbd, scale)


def _conv_sample_kernel(past_ref, x_ref, w_ref, b_ref, o_ref, *, steps):
    ext = [past_ref[i] for i in range(CONV_W - 1)] + [x_ref[t] for t in range(steps)]
    for t in range(steps):
        acc = b_ref[...]
        for j in range(CONV_W):
            acc = acc + ext[t + j] * w_ref[j:j + 1, :]
        o_ref[t] = _silu(acc)


def _conv_sample(past_t, x_t, cw, cb):
    steps, bs, _ = x_t.shape
    return pl.pallas_call(
        functools.partial(_conv_sample_kernel, steps=steps),
        out_shape=jax.ShapeDtypeStruct((steps, bs, CONV_DIM), F32),
        compiler_params=pltpu.CompilerParams(vmem_limit_bytes=VMEM_LIMIT),
    )(past_t, x_t, cw, cb)


def _softplus(x):
    return jnp.maximum(x, 0.0) + jnp.log(1.0 + jnp.exp(-jnp.abs(x)))


def _ssd_prepare(xc, dt_raw, dtb, alog, seq_len):
    q = xc.shape[0]
    dt = _softplus(dt_raw + dtb)
    dta = dt * (-jnp.exp(alog))
    ti = lax.broadcasted_iota(jnp.int32, (q, q), 0)
    si = lax.broadcasted_iota(jnp.int32, (q, q), 1)
    same = (ti // seq_len) == (si // seq_len)
    causal = same & (si <= ti)
    after = same & (si > ti)
    acum = _dot_exact(causal.astype(F32), dta)
    suf = _dot_exact(after.astype(F32), dta)
    return dt.T, acum, acum.T, suf.T, causal


def _ssd_diag_head(h, xc, cb, causal, acum, acum_t, dt_t):
    col = acum[:, h:h + 1]
    row = acum_t[h:h + 1, :]
    decay = jnp.where(causal, jnp.exp(jnp.minimum(col - row, 0.0)), 0.0)
    mat = cb * decay * dt_t[h:h + 1, :]
    xh = xc[:, h * HEAD_DIM:(h + 1) * HEAD_DIM]
    return _dot(mat.astype(BF16), xh.astype(BF16)), xh


def _ssd_gate_norm(y, z, ng):
    return _rms(y * _silu(z), ng)


def _ssd_prompt_kernel(xbc_ref, dt_ref, z_ref, cw_ref, cb_ref, dtb_ref, alog_ref, dskip_ref, ng_ref,
                       y_ref, hout_ref, ext_ref, h_ref, ybuf_ref):
    c = pl.program_id(1)
    q = SSD_CHUNK

    @pl.when(c == 0)
    def _():
        ext_ref[0:SUBLANES, :] = jnp.zeros((SUBLANES, CONV_DIM), F32)
        h_ref[...] = jnp.zeros_like(h_ref)

    ext_ref[SUBLANES:SUBLANES + q, :] = xbc_ref[...]
    acc = cb_ref[...]
    for j in range(CONV_W):
        lo = SUBLANES - (CONV_W - 1) + j
        acc = acc + ext_ref[lo:lo + q, :] * cw_ref[j:j + 1, :]
    xc = _silu(acc)
    ext_ref[0:SUBLANES, :] = ext_ref[q:q + SUBLANES, :]

    dt_t, acum, acum_t, suf_t, causal = _ssd_prepare(xc, dt_ref[...], dtb_ref[...], alog_ref[...], q)
    x_t = xc[:, 0:SSD_WIDTH].T
    for g in range(SSD_GROUPS):
        bg = xc[:, SSD_WIDTH + g * D_STATE:SSD_WIDTH + (g + 1) * D_STATE].astype(BF16)
        cg = xc[:, SSD_WIDTH + (SSD_GROUPS + g) * D_STATE:SSD_WIDTH + (SSD_GROUPS + g + 1) * D_STATE].astype(BF16)
        cb = _dot_nt(cg, bg)
        for r in range(HEADS_PER_GROUP):
            h = g * HEADS_PER_GROUP + r
            yd, xh = _ssd_diag_head(h, xc, cb, causal, acum, acum_t, dt_t)
            hst = h_ref[h]
            yo = _dot_nt(cg, hst.astype(BF16)) * jnp.exp(acum[:, h:h + 1])
            ybuf_ref[:, h * HEAD_DIM:(h + 1) * HEAD_DIM] = yd + yo + dskip_ref[0:1, h:h + 1] * xh
            wrow = jnp.exp(suf_t[h:h + 1, :]) * dt_t[h:h + 1, :]
            st = _dot((x_t[h * HEAD_DIM:(h + 1) * HEAD_DIM, :] * wrow).astype(BF16), bg)
            total = acum_t[h:h + 1, q - 1:q]
            h_ref[h] = jnp.exp(total) * hst + st
    y_ref[...] = _ssd_gate_norm(ybuf_ref[...], z_ref[...], ng_ref[...])
    hout_ref[0] = h_ref[...]


def _ssd_prompt(xbc, dt, z, cw, cb, dtb, alog, dskip, ng, batch, seq):
    q = SSD_CHUNK
    nc = seq // q
    tok = lambda b, c: (b * nc + c, 0)
    const = lambda b, c: (0, 0)
    return pl.pallas_call(
        _ssd_prompt_kernel,
        grid=(batch, nc),
        in_specs=[pl.BlockSpec((q, CONV_DIM), tok),
                  pl.BlockSpec((q, LANES), tok),
                  pl.BlockSpec((q, SSD_WIDTH), tok),
                  pl.BlockSpec((CONV_W, CONV_DIM), const),
                  pl.BlockSpec((1, CONV_DIM), const),
                  pl.BlockSpec((1, LANES), const),
                  pl.BlockSpec((1, LANES), const),
                  pl.BlockSpec((1, LANES), const),
                  pl.BlockSpec((1, SSD_WIDTH), const)],
        out_specs=[pl.BlockSpec((q, SSD_WIDTH), tok),
                   pl.BlockSpec((1, SSD_HEADS, HEAD_DIM, D_STATE), lambda b, c: (b, 0, 0, 0))],
        out_shape=[jax.ShapeDtypeStruct((batch * seq, SSD_WIDTH), F32),
                   jax.ShapeDtypeStruct((batch, SSD_HEADS, HEAD_DIM, D_STATE), F32)],
        scratch_shapes=[pltpu.VMEM((q + 2 * SUBLANES, CONV_DIM), F32),
                        pltpu.VMEM((SSD_HEADS, HEAD_DIM, D_STATE), F32),
                        pltpu.VMEM((q, SSD_WIDTH), F32)],
        compiler_params=_params(("arbitrary", "arbitrary")),
    )(xbc, dt, z, cw, cb, dtb, alog, dskip, ng)


def _ssd_sample_kernel(xc_ref, dt_ref, z_ref, h0_ref, dtb_ref, alog_ref, dskip_ref, ng_ref,
                       y_ref, hout_ref, ybuf_ref, acum_ref, xw_ref, *, steps):
    rows = xc_ref.shape[0]
    xc = xc_ref[...]
    dt_t, acum, acum_t, suf_t, causal = _ssd_prepare(xc, dt_ref[...], dtb_ref[...], alog_ref[...], steps)
    acum_ref[...] = acum
    x_t = xc[:, 0:SSD_WIDTH].T
    for g in range(SSD_GROUPS):
        bg = xc[:, SSD_WIDTH + g * D_STATE:SSD_WIDTH + (g + 1) * D_STATE].astype(BF16)
        cg = xc[:, SSD_WIDTH + (SSD_GROUPS + g) * D_STATE:SSD_WIDTH + (SSD_GROUPS + g + 1) * D_STATE].astype(BF16)
        cb = _dot_nt(cg, bg)
        for r in range(HEADS_PER_GROUP):
            h = g * HEADS_PER_GROUP + r
            yd, xh = _ssd_diag_head(h, xc, cb, causal, acum, acum_t, dt_t)
            ybuf_ref[:, h * HEAD_DIM:(h + 1) * HEAD_DIM] = yd + dskip_ref[0:1, h:h + 1] * xh
            wrow = jnp.exp(suf_t[h:h + 1, :]) * dt_t[h:h + 1, :]
            xw_ref[h * HEAD_DIM:(h + 1) * HEAD_DIM, :] = x_t[h * HEAD_DIM:(h + 1) * HEAD_DIM, :] * wrow

    lane = lax.broadcasted_iota(jnp.int32, (1, rows), 1)

    def per_seq(b, carry):
        r0 = pl.multiple_of(b * steps, steps)
        in_seq = (lane // steps) == b
        ac = acum_ref[pl.ds(r0, steps), :]
        for h in range(SSD_HEADS):
            g = h // HEADS_PER_GROUP
            c_lo = SSD_WIDTH + (SSD_GROUPS + g) * D_STATE
            b_lo = SSD_WIDTH + g * D_STATE
            hst = h0_ref[b, h]
            cg = xc_ref[pl.ds(r0, steps), c_lo:c_lo + D_STATE].astype(BF16)
            yo = _dot_nt(cg, hst.astype(BF16)) * jnp.exp(ac[:, h:h + 1])
            sl = slice(h * HEAD_DIM, (h + 1) * HEAD_DIM)
            ybuf_ref[pl.ds(r0, steps), sl] = ybuf_ref[pl.ds(r0, steps), sl] + yo
            xw = jnp.where(in_seq, xw_ref[sl, :], 0.0).astype(BF16)
            st = _dot(xw, xc_ref[:, b_lo:b_lo + D_STATE].astype(BF16))
            total = ac[steps - 1:steps, h:h + 1]
            hout_ref[b, h] = jnp.exp(total) * hst + st
        return carry

    lax.fori_loop(0, rows // steps, per_seq, 0)
    y_ref[...] = _ssd_gate_norm(ybuf_ref[...], z_ref[...], ng_ref[...])


def _ssd_sample(xc, dt, z, h0, dtb, alog, dskip, ng, row0, steps):
    nseq = h0.shape[0]
    nb = SEQS_PER_SSD_BLOCK
    rows = nb * steps
    blk0 = row0 // rows
    const = lambda i: (0, 0)
    return pl.pallas_call(
        functools.partial(_ssd_sample_kernel, steps=steps),
        grid=(nseq // nb,),
        in_specs=[pl.BlockSpec((rows, CONV_DIM), lambda i: (i, 0)),
                  pl.BlockSpec((rows, LANES), lambda i: (blk0 + i, 0)),
                  pl.BlockSpec((rows, SSD_WIDTH), lambda i: (blk0 + i, 0)),
                  pl.BlockSpec((nb, SSD_HEADS, HEAD_DIM, D_STATE), lambda i: (i, 0, 0, 0)),
                  pl.BlockSpec((1, LANES), const),
                  pl.BlockSpec((1, LANES), const),
                  pl.BlockSpec((1, LANES), const),
                  pl.BlockSpec((1, SSD_WIDTH), const)],
        out_specs=[pl.BlockSpec((rows, SSD_WIDTH), lambda i: (i, 0)),
                   pl.BlockSpec((nb, SSD_HEADS, HEAD_DIM, D_STATE), lambda i: (i, 0, 0, 0))],
        out_shape=[jax.ShapeDtypeStruct((nseq * steps, SSD_WIDTH), F32),
                   jax.ShapeDtypeStruct((nseq, SSD_HEADS, HEAD_DIM, D_STATE), F32)],
        scratch_shapes=[pltpu.VMEM((rows, SSD_WIDTH), F32),
                        pltpu.VMEM((rows, LANES), F32),
                        pltpu.VMEM((SSD_WIDTH, rows), F32)],
        compiler_params=_params(("arbitrary",)),
    )(xc, dt, z, h0, dtb, alog, dskip, ng)


def _attn_prompt_kernel(q_ref, k_ref, v_ref, o_ref, kprev, vprev, acc, ms, ls, s_scr, p_scr, m_scr, v_bf):
    i = pl.program_id(1)
    blk = ATT_BLOCK
    tile = ATT_TILE
    scale = HEAD_DIM ** -0.5

    @pl.when(i == 0)
    def _():
        kprev[...] = jnp.zeros_like(kprev)
        vprev[...] = jnp.zeros_like(vprev)

    qi = lax.broadcasted_iota(jnp.int32, (blk, blk), 0)
    kj = lax.broadcasted_iota(jnp.int32, (blk, blk), 1)
    mask_c = kj <= qi
    mask_p_all = kj >= qi
    lane = lax.broadcasted_iota(jnp.int32, (blk, LANES), 1)
    second_head = lane >= HEAD_DIM
    ones = jnp.ones((blk, LANES), BF16)

    nhead = 2 * ATT_CHUNKS

    def run_pattern(d, first):
        span = d * blk
        nfold = tile // span

        def body(idx, carry):
            jb = idx // d
            r = idx % d
            start = jb * span + r
            if d == 1:
                start = pl.multiple_of(start, blk)
            rows = pl.ds(start, blk, stride=d) if d > 1 else pl.ds(start, blk)
            last = tile - span
            rows_prev_tile = pl.ds(last + r, blk, stride=d) if d > 1 else pl.ds(last, blk)
            mask_p = mask_p_all & ((jb > 0) | (i > 0))

            for c in range(ATT_CHUNKS):
                kp = kprev[c, rows_prev_tile, :]
                vp = vprev[c, rows_prev_tile, :]
                if nfold > 1:
                    pstart = jnp.maximum(start - span, r)
                    if d == 1:
                        pstart = pl.multiple_of(pstart, blk)
                    rows_prev = pl.ds(pstart, blk, stride=d) if d > 1 else pl.ds(pstart, blk)
                    kp = jnp.where(jb > 0, k_ref[c, rows_prev, :], kp)
                    vp = jnp.where(jb > 0, v_ref[c, rows_prev, :], vp)
                qc = q_ref[c, rows, :] * scale
                kc = k_ref[c, rows, :].astype(BF16)
                kp = kp.astype(BF16)
                v_bf[2 * c] = v_ref[c, rows, :].astype(BF16)
                v_bf[2 * c + 1] = vp.astype(BF16)
                for hh in range(2):
                    j = 2 * c + hh
                    own = second_head if hh else ~second_head
                    qm = jnp.where(own, qc, 0.0).astype(BF16)
                    s_scr[2 * j] = jnp.where(mask_c, _dot_nt(qm, kc), NEG)
                    s_scr[2 * j + 1] = jnp.where(mask_p, _dot_nt(qm, kp), NEG)
            for j in range(nhead):
                sc = s_scr[2 * j]
                sp = s_scr[2 * j + 1]
                m = jnp.max(jnp.maximum(sc, sp), axis=-1, keepdims=True)
                m_scr[j] = jnp.broadcast_to(m, (blk, LANES))
                p_scr[2 * j] = jnp.exp(sc - m).astype(BF16)
                p_scr[2 * j + 1] = jnp.exp(sp - m).astype(BF16)
            for c in range(ATT_CHUNKS):
                vc = v_bf[2 * c]
                vp = v_bf[2 * c + 1]
                per_head = []
                for hh in range(2):
                    j = 2 * c + hh
                    pc = p_scr[2 * j]
                    pp = p_scr[2 * j + 1]
                    per_head.append((_dot(pc, vc) + _dot(pp, vp), _dot(pc, ones) + _dot(pp, ones), m_scr[j]))
                o, l, m = (jnp.where(second_head, b, a) for a, b in zip(*per_head))
                if first:
                    acc[c, rows, :] = o
                    ms[c, rows, :] = m
                    ls[c, rows, :] = l
                else:
                    m_old = ms[c, rows, :]
                    m_new = jnp.maximum(m_old, m)
                    w_old = jnp.exp(m_old - m_new)
                    w_new = jnp.exp(m - m_new)
                    acc[c, rows, :] = w_old * acc[c, rows, :] + w_new * o
                    ls[c, rows, :] = w_old * ls[c, rows, :] + w_new * l
                    ms[c, rows, :] = m_new
            return carry

        lax.fori_loop(0, tile // blk, body, 0)

    for n, (_, d) in enumerate(DILATED):
        run_pattern(d, n == 0)
    for c in range(ATT_CHUNKS):
        o_ref[:, c * LANES:(c + 1) * LANES] = acc[c] / ls[c]
    kprev[...] = k_ref[...]
    vprev[...] = v_ref[...]


def _attn_prompt(q3, k3, v3, batch, seq):
    tile = ATT_TILE
    nt = seq // tile
    spec = pl.BlockSpec((ATT_CHUNKS, tile, LANES), lambda b, i: (0, b * nt + i, 0))
    scratch = pltpu.VMEM((ATT_CHUNKS, tile, LANES), F32)
    nhead = 2 * ATT_CHUNKS
    return pl.pallas_call(
        _attn_prompt_kernel,
        grid=(batch, nt),
        in_specs=[spec, spec, spec],
        out_specs=pl.BlockSpec((tile, ATT_WIDTH), lambda b, i: (b * nt + i, 0)),
        out_shape=jax.ShapeDtypeStruct((batch * seq, ATT_WIDTH), F32),
        scratch_shapes=[scratch] * 5 + [
            pltpu.VMEM((2 * nhead, ATT_BLOCK, ATT_BLOCK), F32),
            pltpu.VMEM((2 * nhead, ATT_BLOCK, ATT_BLOCK), BF16),
            pltpu.VMEM((nhead, ATT_BLOCK, LANES), F32),
            pltpu.VMEM((2 * ATT_CHUNKS, ATT_BLOCK, LANES), BF16)],
        compiler_params=_params(("arbitrary", "arbitrary")),
    )(q3, k3, v3)


def _sample_multiplicity(steps, wbuf):
    qpos = wbuf + np.arange(steps)[:, None]
    kpos = np.arange(wbuf + steps)[None, :]
    delta = qpos - kpos
    cnt = np.zeros(delta.shape, np.float32)
    for w, d in DILATED:
        cnt += ((delta >= 0) & (delta % d == 0) & (delta // d <= w // d)).astype(np.float32)
    return cnt


def _attn_sample(q, k, v, cache_k, cache_v, layer, row0, steps, prev_out):
    depth, nseq, wbuf = cache_k.shape[0], cache_k.shape[1], cache_k.shape[2]
    chan_major = lambda c: jnp.transpose(c, (0, 1, 3, 4, 2)).reshape(depth, nseq, ATT_WIDTH, wbuf)
    ck = chan_major(cache_k)
    cv = chan_major(cache_v)
    t = q.shape[1]
    view = lambda a: a.reshape(ATT_CHUNKS, t // steps, steps, LANES)
    b0 = row0 // steps
    new = pl.BlockSpec((ATT_CHUNKS, 1, steps, LANES), lambda b: (0, b0 + b, 0, 0))
    window = pl.BlockSpec((1, 1, ATT_WIDTH, wbuf), lambda b: (layer, b, 0, 0))
    cnt = np.tile(_sample_multiplicity(steps, wbuf), (ATT_HEADS, 1))
    cntc = jnp.asarray(cnt[:, :wbuf])
    cntn = jnp.asarray(np.pad(cnt[:, wbuf:], ((0, 0), (0, LANES - steps))))
    rows = ATT_HEADS * steps
    in_specs = [new, new, new, window, window,
                pl.BlockSpec((rows, wbuf), lambda b: (0, 0)),
                pl.BlockSpec((rows, LANES), lambda b: (0, 0))]
    args = [view(q), view(k), view(v), ck, cv, cntc, cntn]
    aliases = {}
    if prev_out is not None:
        in_specs += [pl.BlockSpec(memory_space=pl.ANY)] * 2
        args += list(prev_out)
        aliases = {len(args) - 2: 1, len(args) - 1: 2}
    stacked = jax.ShapeDtypeStruct((depth, nseq, ATT_WIDTH, wbuf), F32)
    return pl.pallas_call(
        functools.partial(_attn_sample_kernel, steps=steps, wbuf=wbuf),
        grid=(nseq,),
        in_specs=in_specs,
        out_specs=[pl.BlockSpec((1, steps, ATT_WIDTH), lambda b: (b, 0, 0)), window, window],
        out_shape=[jax.ShapeDtypeStruct((nseq, steps, ATT_WIDTH), F32), stacked, stacked],
        input_output_aliases=aliases,
        compiler_params=_params(("arbitrary",)),
    )(*args)


def _attn_sample_kernel(*refs, steps, wbuf):
    q_ref, kn_ref, vn_ref, kc_ref, vc_ref, cntc_ref, cntn_ref = refs[:7]
    att_ref, ko_ref, vo_ref = refs[-3:]
    rows = ATT_HEADS * steps
    full = lambda ref: jnp.concatenate([ref[c, 0] for c in range(ATT_CHUNKS)], axis=1)
    q = full(q_ref) * (HEAD_DIM ** -0.5)
    q6 = jnp.concatenate([q] * ATT_HEADS, axis=0)
    row = lax.broadcasted_iota(jnp.int32, (rows, ATT_WIDTH), 0)
    lane = lax.broadcasted_iota(jnp.int32, (rows, ATT_WIDTH), 1)
    own = (lane // HEAD_DIM) == (row // steps)
    qbd = jnp.where(own, q6, 0.0).astype(BF16)
    kc = kc_ref[0, 0]
    vc = vc_ref[0, 0]
    zpad = jnp.zeros((LANES - steps, ATT_WIDTH), F32)
    knp = jnp.concatenate([full(kn_ref), zpad], axis=0)
    vnp = jnp.concatenate([full(vn_ref), zpad], axis=0)
    kn_t = knp.T
    vn_t = vnp.T
    cntc = cntc_ref[...]
    cntn = cntn_ref[...]
    s_c = jnp.where(cntc > 0.0, _dot(qbd, kc.astype(BF16)), NEG)
    s_n = jnp.where(cntn > 0.0, _dot(qbd, kn_t.astype(BF16)), NEG)
    m = jnp.maximum(jnp.max(s_c, axis=-1, keepdims=True), jnp.max(s_n, axis=-1, keepdims=True))
    p_c = cntc * jnp.exp(s_c - m)
    p_n = cntn * jnp.exp(s_n - m)
    l = jnp.sum(p_c, axis=-1, keepdims=True) + jnp.sum(p_n, axis=-1, keepdims=True)
    o = _dot_nt(p_c.astype(BF16), vc.astype(BF16)) + _dot(p_n.astype(BF16), vnp.astype(BF16))
    o = jnp.where(own, o / l, 0.0)
    att_ref[0] = jnp.sum(o.reshape(ATT_HEADS, steps, ATT_WIDTH), axis=0)

    tail = lax.broadcasted_iota(jnp.int32, (ATT_WIDTH, LANES), 1) >= LANES - steps
    for old, new_t, out_ref in ((kc, kn_t, ko_ref), (vc, vn_t, vo_ref)):
        moved = pltpu.roll(old, wbuf - steps, axis=1)
        out_ref[0, 0, :, 0:wbuf - LANES] = moved[:, 0:wbuf - LANES]
        out_ref[0, 0, :, wbuf - LANES:wbuf] = jnp.where(tail, pltpu.roll(new_t, LANES - steps, axis=1),
                                                        moved[:, wbuf - LANES:wbuf])


def _router_kernel(lg_ref, sel_ref, gate_ref, cnt_ref, run_ref):
    i = pl.program_id(0)

    @pl.when(i == 0)
    def _():
        run_ref[...] = jnp.zeros_like(run_ref)

    tm = lg_ref.shape[0]
    work = lg_ref[...]
    lane = lax.broadcasted_iota(jnp.int32, (tm, LANES), 1)
    vals, hots = [], []
    for _ in range(TOP_K):
        m = jnp.max(work, axis=-1, keepdims=True)
        idx = jnp.min(jnp.where(work == m, lane, LANES), axis=-1, keepdims=True)
        hot = lane == idx
        vals.append(m)
        hots.append(hot)
        work = jnp.where(hot, -jnp.inf, work)
    exps = [jnp.exp(v - vals[0]) for v in vals]
    den = exps[0]
    for e in exps[1:]:
        den = den + e
    chosen = hots[0]
    for hot in hots[1:]:
        chosen = chosen | hot
    onehot = chosen.astype(F32)
    ri = lax.broadcasted_iota(jnp.int32, (tm, tm), 0)
    ci = lax.broadcasted_iota(jnp.int32, (tm, tm), 1)
    before = (ci < ri).astype(BF16)
    rank = _dot(before, onehot.astype(BF16)) + run_ref[...]
    sel = jnp.zeros((tm, LANES), jnp.int32)
    gates = jnp.zeros((tm, LANES), F32)
    for k in range(TOP_K):
        e_k = jnp.min(jnp.where(hots[k], lane, LANES), axis=-1, keepdims=True)
        r_k = jnp.sum(jnp.where(hots[k], rank, 0.0), axis=-1, keepdims=True).astype(jnp.int32)
        sel = jnp.where(lane == k, e_k, sel)
        sel = jnp.where(lane == TOP_K + k, r_k, sel)
        gates = jnp.where(lane == k, exps[k] / den, gates)
    sel_ref[...] = sel
    gate_ref[...] = gates
    run_ref[...] = run_ref[...] + jnp.sum(onehot, axis=0, keepdims=True)
    cnt_ref[...] = run_ref[...]


def _router(logits):
    t = logits.shape[0]
    tm = TOKEN_TILE
    tok = lambda i: (i, 0)
    return pl.pallas_call(
        _router_kernel,
        grid=(t // tm,),
        in_specs=[pl.BlockSpec((tm, LANES), tok)],
        out_specs=[pl.BlockSpec((tm, LANES), tok), pl.BlockSpec((tm, LANES), tok),
                   pl.BlockSpec((1, LANES), lambda i: (0, 0))],
        out_shape=[jax.ShapeDtypeStruct((t, LANES), jnp.int32), jax.ShapeDtypeStruct((t, LANES), F32),
                   jax.ShapeDtypeStruct((1, LANES), F32)],
        scratch_shapes=[pltpu.VMEM((1, LANES), F32)],
        compiler_params=_params(("arbitrary",)),
    )(logits)


def _row_copy(src, dst, sem):
    return pltpu.make_async_copy(src, dst, sem)


ROWS_PER_ISSUE = 8

ROW_TILES = D_MODEL // LANES
assert ROW_TILES == SUBLANES


def _store_rows_as_tiles(ref, x):
    n = x.shape[0]
    for s in range(ROW_TILES):
        ref[pl.ds(s, n, stride=ROW_TILES), :] = x[:, s * LANES:(s + 1) * LANES]


def _tile_rows_chunk(ref, s, n, lead=()):
    return ref[lead + (pl.ds(s, n, stride=ROW_TILES), slice(None))]


def _tile_of_row(ref, r, lead=()):
    start = r * ROW_TILES
    if not isinstance(start, int):
        start = pl.multiple_of(start, ROW_TILES)
    return ref.at[lead + (pl.ds(start, ROW_TILES),)]


def _dispatch_kernel(dest_ref, fill_ref, nu_ref, h2_ref, xg_ref, zero_ref, sem, zsem, *, n_blocks):
    i = pl.program_id(0)
    tm = h2_ref.shape[0] // ROW_TILES
    tb = EXPERT_BLOCK

    @pl.when(i == 0)
    def _():
        zero_ref[...] = jnp.zeros_like(zero_ref)

        def fill_copy(start):
            rows = tb * ROW_TILES
            return pltpu.make_async_copy(zero_ref, xg_ref.at[pl.ds(pl.multiple_of(start * ROW_TILES, rows), rows)],
                                         zsem)

        def each_fill(fn):
            def per_expert(e, carry):
                @pl.when(fill_ref[e] >= 0)
                def _():
                    fn(fill_copy(fill_ref[e]))
                return carry

            def per_tail(b, carry):
                fn(fill_copy(b * tb))
                return carry

            lax.fori_loop(0, N_EXPERTS, per_expert, 0)
            lax.fori_loop(nu_ref[0], n_blocks, per_tail, 0)

        each_fill(lambda cp: cp.start())
        each_fill(lambda cp: cp.wait())

    def issue(g, carry):
        for u in range(ROWS_PER_ISSUE):
            r = g * ROWS_PER_ISSUE + u
            for k in range(TOP_K):
                d = dest_ref[(i * tm + r) * TOP_K + k]
                _row_copy(_tile_of_row(h2_ref, r), _tile_of_row(xg_ref, d), sem).start()
        return carry

    lax.fori_loop(0, tm // ROWS_PER_ISSUE, issue, 0)

    def drain(g, carry):
        for _ in range(ROWS_PER_ISSUE * TOP_K):
            _row_copy(_tile_of_row(h2_ref, 0), _tile_of_row(xg_ref, 0), sem).wait()
        return carry

    lax.fori_loop(0, tm // ROWS_PER_ISSUE, drain, 0)


def _dispatch(dest_flat, fill_start, n_used, h2_tiles, n_blocks):
    t = h2_tiles.shape[0] // ROW_TILES
    tm = TOKEN_TILE
    tb = EXPERT_BLOCK
    return pl.pallas_call(
        functools.partial(_dispatch_kernel, n_blocks=n_blocks),
        grid_spec=pltpu.PrefetchScalarGridSpec(
            num_scalar_prefetch=3,
            grid=(t // tm,),
            in_specs=[pl.BlockSpec((tm * ROW_TILES, LANES), lambda i, d, f, n: (i, 0))],
            out_specs=pl.BlockSpec(memory_space=pl.ANY),
            scratch_shapes=[pltpu.VMEM((tb * ROW_TILES, LANES), F32), pltpu.SemaphoreType.DMA(()),
                            pltpu.SemaphoreType.DMA(())]),
        out_shape=jax.ShapeDtypeStruct((n_blocks * tb * ROW_TILES, LANES), F32),
        compiler_params=_params(("arbitrary",)),
    )(dest_flat, fill_start, n_used, h2_tiles)


_CAST_ROWS = 128


def _expert_kernel(be_ref, nu_ref, x_ref, wgu_ref, bgu_ref, wd_ref, bd_ref, y_ref, wgu_bf, wd_bf, x_bf):
    i = pl.program_id(0)
    tb = EXPERT_BLOCK

    @pl.when(i < nu_ref[0])
    def _():
        @pl.when((i == 0) | (be_ref[i] != be_ref[jnp.maximum(i - 1, 0)]))
        def _():
            def cast(c, carry):
                r = pl.multiple_of(c * _CAST_ROWS, _CAST_ROWS)
                wgu_bf[pl.ds(r, _CAST_ROWS), :] = wgu_ref[0, 0, pl.ds(r, _CAST_ROWS), :].astype(BF16)
                wd_bf[pl.ds(r, _CAST_ROWS), :] = wd_ref[0, 0, pl.ds(r, _CAST_ROWS), :].astype(BF16)
                return carry

            lax.fori_loop(0, D_MODEL // _CAST_ROWS, cast, 0)

        for s in range(ROW_TILES):
            x_bf[:, s * LANES:(s + 1) * LANES] = _tile_rows_chunk(x_ref, s, tb).astype(BF16)
        gu = _dot(x_bf[...], wgu_bf[...]) + bgu_ref[0, 0]
        gate = jnp.minimum(gu[:, :D_FF], SWIGLU_LIMIT)
        up = jnp.clip(gu[:, D_FF:], -SWIGLU_LIMIT, SWIGLU_LIMIT)
        act = (up + 1.0) * gate * _sigmoid(SWIGLU_ALPHA * gate)
        _store_rows_as_tiles(y_ref, _dot(act.astype(BF16), wd_bf[...]) + bd_ref[0, 0])

    @pl.when(i >= nu_ref[0])
    def _():
        y_ref[...] = jnp.zeros_like(y_ref)


def _experts(block_e, n_used, xg, wgu, bgu, wd, bd, layer):
    assert D_FF == D_MODEL
    n_rows = xg.shape[0] // ROW_TILES
    tb = EXPERT_BLOCK
    blk = lambda i, be, nu: (jnp.minimum(i, nu[0] - 1), 0)
    exp4 = lambda i, be, nu: (layer, be[i], 0, 0)
    return pl.pallas_call(
        _expert_kernel,
        grid_spec=pltpu.PrefetchScalarGridSpec(
            num_scalar_prefetch=2,
            grid=(n_rows // tb,),
            in_specs=[pl.BlockSpec((tb * ROW_TILES, LANES), blk),
                      pl.BlockSpec((1, 1, D_MODEL, 2 * D_FF), exp4),
                      pl.BlockSpec((1, 1, 1, 2 * D_FF), exp4),
                      pl.BlockSpec((1, 1, D_FF, D_MODEL), exp4),
                      pl.BlockSpec((1, 1, 1, D_MODEL), exp4)],
            out_specs=pl.BlockSpec((tb * ROW_TILES, LANES), lambda i, be, nu: (i, 0)),
            scratch_shapes=[pltpu.VMEM((D_MODEL, 2 * D_FF), BF16), pltpu.VMEM((D_FF, D_MODEL), BF16),
                            pltpu.VMEM((tb, D_MODEL), BF16)]),
        out_shape=jax.ShapeDtypeStruct((n_rows * ROW_TILES, LANES), F32),
        compiler_params=_params(("arbitrary",), vmem=EXPERT_VMEM_LIMIT),
    )(block_e, n_used, xg, wgu, bgu, wd, bd)


def _combine_kernel(dest_ref, yg_ref, x1_ref, gate_ref, gt_ref, o_ref, buf_ref, sem):
    i = pl.program_id(0)
    tm = x1_ref.shape[0]

    def issue(g, carry):
        for u in range(ROWS_PER_ISSUE):
            r = g * ROWS_PER_ISSUE + u
            for k in range(TOP_K):
                d = dest_ref[(i * tm + r) * TOP_K + k]
                _row_copy(_tile_of_row(yg_ref, d), _tile_of_row(buf_ref, r, (k,)), sem).start()
        return carry

    lax.fori_loop(0, tm // ROWS_PER_ISSUE, issue, 0)

    def drain(g, carry):
        for _ in range(ROWS_PER_ISSUE * TOP_K):
            _row_copy(_tile_of_row(yg_ref, 0), _tile_of_row(buf_ref, 0, (0,)), sem).wait()
        return carry

    lax.fori_loop(0, tm // ROWS_PER_ISSUE, drain, 0)
    gates = gate_ref[...]
    for s in range(ROW_TILES):
        ff = gates[:, 0:1] * _tile_rows_chunk(buf_ref, s, tm, (0,))
        for k in range(1, TOP_K):
            ff = ff + gates[:, k:k + 1] * _tile_rows_chunk(buf_ref, s, tm, (k,))
        sl = slice(s * LANES, (s + 1) * LANES)
        o_ref[:, sl] = x1_ref[:, sl] + gt_ref[:, sl] * ff


def _combine(dest_flat, yg, x1, gates, gtx, mod_map):
    t = x1.shape[0]
    tm = TOKEN_TILE
    tok = lambda i, d: (i, 0)
    return pl.pallas_call(
        _combine_kernel,
        grid_spec=pltpu.PrefetchScalarGridSpec(
            num_scalar_prefetch=1,
            grid=(t // tm,),
            in_specs=[pl.BlockSpec(memory_space=pl.ANY),
                      pl.BlockSpec((tm, D_MODEL), tok),
                      pl.BlockSpec((tm, LANES), tok),
                      pl.BlockSpec((None, tm, D_MODEL), lambda i, d: mod_map(MOD_GT2)(i))],
            out_specs=pl.BlockSpec((tm, D_MODEL), tok),
            scratch_shapes=[pltpu.VMEM((TOP_K, tm * ROW_TILES, LANES), F32), pltpu.SemaphoreType.DMA(())]),
        out_shape=jax.ShapeDtypeStruct((t, D_MODEL), F32),
        compiler_params=_params(("arbitrary",)),
    )(dest_flat, yg, x1, gates, gtx)


def _route_tables(sel, counts, n_blocks):
    tb = EXPERT_BLOCK
    cnt = counts[0, :N_EXPERTS].astype(jnp.int32)
    padded = (cnt + tb - 1) // tb * tb
    ends = jnp.cumsum(padded)
    starts = ends - padded
    experts = sel[:, :TOP_K]
    ranks = sel[:, TOP_K:2 * TOP_K]
    eid = jnp.arange(N_EXPERTS, dtype=jnp.int32)
    dest = (ranks + jnp.sum(jnp.where(experts[..., None] == eid, starts, 0), axis=-1)).reshape(-1)
    blk_start = jnp.arange(n_blocks, dtype=jnp.int32) * tb
    block_e = jnp.minimum(jnp.sum((ends[None, :] <= blk_start[:, None]).astype(jnp.int32), axis=1), N_EXPERTS - 1)
    n_used = (ends[-1:] // tb).astype(jnp.int32)
    fill_start = jnp.where(cnt % tb != 0, ends - tb, -1)
    return dest.astype(jnp.int32), block_e.astype(jnp.int32), n_used, fill_start.astype(jnp.int32)


def _pad_lanes(v, fill=0.0):
    v = v.reshape(1, -1).astype(F32)
    return jnp.pad(v, ((0, 0), (0, LANES - v.shape[1])), constant_values=fill)


def _pack_w_in(w):
    cuts = np.cumsum([0, POOL_WIDTH, SSD_WIDTH, CONV_DIM, SSD_HEADS, ATT_WIDTH, ATT_WIDTH, ATT_WIDTH])
    pieces = []
    for i, (_, width) in enumerate(_IN_COLS):
        piece = w[:, cuts[i]:cuts[i + 1]]
        pieces.append(jnp.pad(piece, ((0, 0), (0, width - piece.shape[1]))))
    return jnp.concatenate(pieces, axis=1).astype(BF16)


def _block_diag(pool_w):
    g, c, _ = pool_w.shape
    out = jnp.zeros((g * c, g * c), F32)
    for i in range(g):
        out = out.at[i * c:(i + 1) * c, i * c:(i + 1) * c].set(pool_w[i])
    return out.astype(BF16)


def kernel(x_prompt, x_sample, state_pool, state_conv, state_ssm, cache_k, cache_v, c_prompt, c_sample, g_mix, w_ada, b_ada, w_in, pool_w, pool_scale, conv_w, conv_b, dt_bias, a_log, d_skip, ssd_norm_g, w_out, g_ffn, router_w, router_b, w_gate_up, b_gate_up, w_down, b_down, g_final):
    batch, seq, _ = x_prompt.shape
    nseq, steps, _ = x_sample.shape
    depth = w_in.shape[0]
    wbuf = cache_k.shape[2]
    tp = batch * seq
    ts = nseq * steps
    t = tp + ts
    tm = TOKEN_TILE
    assert steps == SUBLANES and seq % (tm * 2) == 0 and ts % tm == 0 and nseq % SEQS_PER_SSD_BLOCK == 0
    assert seq % ATT_TILE == 0

    c_all = jnp.concatenate([c_prompt, c_sample], axis=0)
    pad_rows = (-c_all.shape[0]) % SUBLANES
    c_all = jnp.pad(c_all, ((0, pad_rows), (0, 0)))
    mod = _ada(c_all, w_ada, b_ada)

    tiles_per_seq = seq // tm
    n_prompt_tiles = tp // tm

    modx = jnp.concatenate([jnp.repeat(mod[:, :batch], tm, axis=1),
                            jnp.repeat(mod[:, batch:batch + nseq], steps, axis=1)], axis=1)

    def mod_row(i):
        return jnp.where(i < n_prompt_tiles, i // tiles_per_seq, batch + i - n_prompt_tiles)

    x = jnp.concatenate([x_prompt.reshape(tp, D_MODEL), x_sample.reshape(ts, D_MODEL)], axis=0)

    n_blocks = -(-(t * TOP_K) // EXPERT_BLOCK) + N_EXPERTS

    pool_p, pool_s, conv_p, conv_s, ssm_p, ssm_s, k_p, v_p = [], [], [], [], [], [], [], []
    kv_s = None
    for l in range(depth):
        mod_map = lambda vec, l=l: (lambda i: (l, mod_row(i), vec))
        u, z, xbc, dt, q, k, v = _inproj(x, modx, modx, g_mix[l].reshape(1, -1), _pack_w_in(w_in[l]), mod_map)

        wbd = _block_diag(pool_w[l])
        pscale = pool_scale[l].reshape(1, -1)
        cw, cb = conv_w[l], conv_b[l].reshape(1, -1)
        dtb, alog, dskip = _pad_lanes(dt_bias[l]), _pad_lanes(a_log[l]), _pad_lanes(d_skip[l])
        ng = ssd_norm_g[l].reshape(1, -1)

        pool_out_p, pool_last = _pool_prompt(u, wbd, pscale, batch, seq)
        ssd_out_p, ssm_new_p = _ssd_prompt(xbc, dt, z, cw, cb, dtb, alog, dskip, ng, batch, seq)
        att_p = _attn_prompt(q, k, v, batch, seq)

        u_s = u[tp:].reshape(nseq, steps, POOL_WIDTH)
        xbc_s = xbc[tp:].reshape(nseq, steps, CONV_DIM)
        tl = lambda a: jnp.swapaxes(a, 0, 1)
        pool_out_s = tl(_pool_sample(tl(state_pool[l]), tl(u_s), wbd, pscale, wbuf)).reshape(ts, POOL_WIDTH)
        xc_s = tl(_conv_sample(tl(state_conv[l]), tl(xbc_s), cw, cb)).reshape(ts, CONV_DIM)
        ssd_out_s, ssm_new_s = _ssd_sample(xc_s, dt, z, state_ssm[l], dtb, alog, dskip, ng, tp, steps)
        att_s, ks_new, vs_new = _attn_sample(q, k, v, cache_k, cache_v, l, tp, steps, kv_s)
        kv_s = (ks_new, vs_new)

        pool_all = (pool_out_p, pool_out_s)
        ssd_all = (ssd_out_p, ssd_out_s)
        att_all = (att_p, att_s.reshape(ts, ATT_WIDTH))

        rw = jnp.pad(router_w[l], ((0, 0), (0, LANES - N_EXPERTS)))
        rb = _pad_lanes(router_b[l], fill=NEG)
        x1, h2, logits = _mixout(pool_all, ssd_all, att_all, x, modx, modx, modx, g_ffn[l].reshape(1, -1),
                                 w_out[l].astype(BF16), rw, rb, mod_map, n_prompt_tiles)
        sel, gates, counts = _router(logits)
        dest, block_e, n_used, fill_start = _route_tables(sel, counts, n_blocks)
        xg = _dispatch(dest, fill_start, n_used, h2, n_blocks)
        yg = _experts(block_e, n_used, xg, w_gate_up, b_gate_up.reshape(depth, N_EXPERTS, 1, -1),
                      w_down, b_down.reshape(depth, N_EXPERTS, 1, -1), l)
        x = _combine(dest, yg, x1, gates, modx, mod_map)

        pool_p.append(pool_last[:, _POOL_CARRY - POOL_KEEP:])
        pool_s.append(jnp.concatenate([state_pool[l], u_s], axis=1)[:, -POOL_KEEP:])
        conv_p.append(jnp.stack([xbc[(b + 1) * seq - (CONV_W - 1):(b + 1) * seq] for b in range(batch)], axis=0))
        conv_s.append(jnp.concatenate([state_conv[l], xbc_s], axis=1)[:, -(CONV_W - 1):])
        ssm_p.append(ssm_new_p)
        ssm_s.append(ssm_new_s)
        keep = min(MAX_WINDOW, seq)
        kept = lambda a: jnp.transpose(
            jnp.stack([a[:, (b + 1) * seq - keep:(b + 1) * seq] for b in range(batch)], axis=0),
            (0, 2, 1, 3)).reshape(batch, keep, ATT_HEADS, HEAD_DIM)
        k_p.append(kept(k))
        v_p.append(kept(v))

    y_prompt = _final_norm(x, g_final.reshape(1, -1), 0, tp).reshape(batch, seq, D_MODEL)
    y_sample = _final_norm(x, g_final.reshape(1, -1), tp, ts).reshape(nseq, steps, D_MODEL)
    pos_major = lambda c: jnp.transpose(c.reshape(depth, nseq, ATT_HEADS, HEAD_DIM, wbuf), (0, 1, 4, 2, 3))
    k_s = pos_major(kv_s[0])
    v_s = pos_major(kv_s[1])
    st = lambda xs: jnp.stack(xs, axis=0)
    return (y_prompt, y_sample, st(pool_p), st(pool_s), st(conv_p), st(conv_s), st(ssm_p), st(ssm_s),
            st(k_p), k_s, st(v_p), v_s)
```
